```python
import math
import jax
import jax.numpy as jnp
from jax import lax
import numpy as np

D_MODEL = 2048
BATCH = 2
SEQ = 4096
DEPTH = 2
DEC_BATCH = 128
DEC_SEQ = 1
PAST_LEN = 2048
PAGE_SIZE = 128

HG_DK = 128
HG_HEADS = D_MODEL // HG_DK
HG_DV = D_MODEL // HG_HEADS
HG_CHUNK = 64
HEAD_DIM = 128
NSA_HEADS = D_MODEL // HEAD_DIM
NSA_KV_HEADS = 4
NSA_GROUP = NSA_HEADS // NSA_KV_HEADS
CMP_STRIDE = 16
CMP_BLOCK = 2 * CMP_STRIDE
SEL_BLOCK = 64
N_SELECT = 16
WINDOW = 512
Q_BLOCK = 128
N_BUCKETS = 32
MAX_DISTANCE = 128
D_FF = -(-(8 * D_MODEL) // (3 * 256)) * 256
KV_WIDTH = NSA_KV_HEADS * HEAD_DIM
IN_SIZES = (HG_HEADS * HG_DK, HG_HEADS * HG_DK, HG_HEADS * HG_DV, HG_HEADS * HG_DV,
            NSA_HEADS * HEAD_DIM, KV_WIDTH, KV_WIDTH, KV_WIDTH, KV_WIDTH, KV_WIDTH, KV_WIDTH,
            NSA_HEADS * 3, D_MODEL, D_MODEL)
IN_WIDTH = sum(IN_SIZES)
EPS = 1e-6
NEG = -1e30
SELECT_BONUS = 1e3

kernel_name = 'hgrn2_nsa_parallel_decoder_step'


def rms_norm(x, gain):
    xf = x.astype(jnp.float32)
    y = xf * lax.rsqrt(jnp.mean(xf * xf, axis=-1, keepdims=True) + EPS)
    return (y * gain.astype(jnp.float32)).astype(x.dtype)


def ada(c, w, b):
    return jnp.split(jax.nn.silu(c) @ w + b, 6, axis=-1)


def modulate(x, gain, shift, scale):
    return rms_norm(x, gain) * (1.0 + scale[:, None, :]) + shift[:, None, :]


def split_cols(z):
    parts, start = [], 0
    for size in IN_SIZES:
        parts.append(z[..., start:start + size])
        start += size
    return parts


def swiglu(h, w1, w2):
    gate, up = jnp.split(h @ w1, 2, axis=-1)
    return (jax.nn.silu(gate) * up) @ w2


def t5_bucket(dist):
    n = jnp.maximum(dist, 0)
    max_exact = N_BUCKETS // 2
    nf = jnp.maximum(n, 1).astype(jnp.float32)
    large = max_exact + (jnp.log(nf / max_exact) / math.log(MAX_DISTANCE / max_exact)
                         * (N_BUCKETS - max_exact)).astype(jnp.int32)
    return jnp.where(n < max_exact, n, jnp.minimum(large, N_BUCKETS - 1))


def masked_softmax(s, mask):
    s = jnp.where(mask, s, NEG)
    m = jnp.max(s, axis=-1, keepdims=True)
    p = jnp.where(mask, jnp.exp(s - m), 0.0)
    return p / jnp.maximum(jnp.sum(p, axis=-1, keepdims=True), 1e-30)


def hgrn2_scan(q, k, v, logf, s0):
    n, t = q.shape[:2]
    c = HG_CHUNK if t % HG_CHUNK == 0 else t
    nc = t // c
    tril = jnp.tril(jnp.ones((c, c), bool))

    def to_chunks(a):
        return a.reshape(n, nc, c, *a.shape[2:]).swapaxes(0, 1)

    def step(s, inp):
        qc, kc, vc, gc = inp
        b = jnp.cumsum(gc, axis=1)
        diff = b[:, :, None] - b[:, None, :]
        decay = jnp.exp(jnp.where(tril[None, :, :, None, None], diff, NEG))
        a = jnp.einsum('ntshc,nthc,nshc->nhts', decay, qc, kc)
        o = jnp.einsum('nhts,nshv->nthv', a, vc) + jnp.einsum('nthc,nhcv->nthv', qc * jnp.exp(b), s)
        b_last = b[:, -1]
        s_new = jnp.exp(b_last)[..., None] * s + jnp.einsum(
            'nshc,nshv->nhcv', kc * jnp.exp(b_last[:, None] - b), vc)
        return s_new, o

    s_fin, o = lax.scan(step, s0, (to_chunks(q), to_chunks(k), to_chunks(v), to_chunks(logf)))
    return o.swapaxes(0, 1).reshape(n, t, *o.shape[3:]), s_fin


def hgrn2_branch(zq, zf, zi, zg, lower, gain, s0):
    f32 = jnp.float32
    n, t = zq.shape[:2]
    kshape = (n, t, HG_HEADS, HG_DK)
    vshape = (n, t, HG_HEADS, HG_DV)
    f = lower + (1.0 - lower) * jax.nn.sigmoid(zf.astype(f32))
    o, s = hgrn2_scan(zq.astype(f32).reshape(kshape), (1.0 - f).reshape(kshape),
                      zi.astype(f32).reshape(vshape), jnp.log(f).reshape(kshape), s0.astype(f32))
    o = rms_norm(o, gain) * jax.nn.silu(zg.astype(f32).reshape(vshape))
    return o.reshape(n, t, HG_HEADS * HG_DV).astype(zq.dtype), s


def compress(rows, w):
    n, t = rows.shape[:2]
    n_half = -(-t // CMP_STRIDE)
    rows = jnp.pad(rows, ((0, 0), (0, n_half * CMP_STRIDE - t), (0, 0), (0, 0)))
    halves = rows.reshape(n, n_half, CMP_STRIDE, NSA_KV_HEADS, HEAD_DIM)
    first = jnp.einsum('njpkd,kpde->njke', halves[:, :-1], w[:, 0])
    second = jnp.einsum('njpkd,kpde->njke', halves[:, 1:], w[:, 1])
    return first + second


def nsa_core(q, qpos, kc, vc, cend, n_slc, gather_sel, kw, vw, kwpos, gates, rel_table):
    f32 = jnp.float32
    n, tq = q.shape[:2]
    scale = HEAD_DIM ** -0.5
    table = rel_table.astype(f32)

    def pair_bias(dist):
        b = table[t5_bucket(dist)]
        return b.reshape(*dist.shape, NSA_KV_HEADS, NSA_GROUP).transpose(2, 3, 0, 1)

    d_c = qpos[:, None] - cend[None, :]
    s_c = jnp.einsum('nqkgd,njkd->nkgqj', q, kc).astype(f32) * scale + pair_bias(d_c)
    p_c = masked_softmax(s_c, d_c >= 0)
    o_cmp = jnp.einsum('nkgqj,njkd->nqkgd', p_c.astype(vc.dtype), vc)

    c_start = jnp.arange(kc.shape[1])[:, None] * CMP_STRIDE
    s_start = jnp.arange(n_slc)[None, :] * SEL_BLOCK
    overlap = ((c_start < s_start + SEL_BLOCK) & (c_start + CMP_BLOCK > s_start)).astype(f32)
    imp = jnp.einsum('nkgqj,js->nqks', p_c, overlap)
    blk = jnp.arange(n_slc)[None, :]
    cur = (qpos // SEL_BLOCK)[:, None]
    reach = blk * SEL_BLOCK <= qpos[:, None]
    forced = (blk == 0) | (blk == cur) | (blk == cur - 1)
    score = jnp.where(reach[:, None, :], imp + SELECT_BONUS * forced[:, None, :].astype(f32), NEG)
    _, idx = lax.top_k(score, min(N_SELECT, n_slc))

    k_sel, v_sel = gather_sel(idx)
    n_keys = idx.shape[-1] * SEL_BLOCK
    k_sel = k_sel.reshape(n, tq, NSA_KV_HEADS, n_keys, HEAD_DIM)
    v_sel = v_sel.reshape(n, tq, NSA_KV_HEADS, n_keys, HEAD_DIM)
    kpos = (idx[..., None] * SEL_BLOCK + jnp.arange(SEL_BLOCK)).reshape(n, tq, NSA_KV_HEADS, n_keys)
    d_s = qpos[None, :, None, None] - kpos
    kv_idx = jnp.arange(NSA_KV_HEADS)[None, None, :, None]
    b_s = table.reshape(N_BUCKETS, NSA_KV_HEADS, NSA_GROUP)[t5_bucket(d_s), kv_idx]
    s_s = jnp.einsum('nqkgd,nqksd->nqkgs', q, k_sel).astype(f32) * scale + jnp.swapaxes(b_s, -1, -2)
    p_s = masked_softmax(s_s, (d_s >= 0)[:, :, :, None, :])
    o_sel = jnp.einsum('nqkgs,nqksd->nqkgd', p_s.astype(v_sel.dtype), v_sel)

    d_w = qpos[:, None] - kwpos[None, :]
    s_w = jnp.einsum('nqkgd,nskd->nkgqs', q, kw).astype(f32) * scale + pair_bias(d_w)
    p_w = masked_softmax(s_w, (d_w >= 0) & (d_w < WINDOW) & (kwpos[None, :] >= 0))
    o_win = jnp.einsum('nkgqs,nskd->nqkgd', p_w.astype(vw.dtype), vw)

    g = jax.nn.sigmoid(gates.astype(f32)).reshape(n, tq, NSA_KV_HEADS, NSA_GROUP, 3)
    o = g[..., 0:1] * o_cmp + g[..., 1:2] * o_sel + g[..., 2:3] * o_win
    return o.reshape(n, tq, NSA_HEADS * HEAD_DIM).astype(q.dtype)


def nsa_prompt(q, kc_rows, vc_rows, ks, vs, kw, vw, gates, w_phi_l, gk_cmp, rel_table):
    b, t = q.shape[:2]
    kc = rms_norm(compress(kc_rows, w_phi_l[0]), gk_cmp)
    vc = compress(vc_rows, w_phi_l[1])
    cend = jnp.arange(kc.shape[1]) * CMP_STRIDE + CMP_BLOCK - 1
    n_slc = t // SEL_BLOCK

    def blocks(rows):
        return rows.reshape(b, n_slc, SEL_BLOCK, NSA_KV_HEADS, HEAD_DIM).transpose(0, 3, 1, 2, 4)

    kb, vb = blocks(ks), blocks(vs)
    bi = jnp.arange(b)[:, None, None, None]
    hi = jnp.arange(NSA_KV_HEADS)[None, None, :, None]

    def gather_sel(idx):
        return kb[bi, hi, idx], vb[bi, hi, idx]

    pad = ((0, 0), (WINDOW, 0), (0, 0), (0, 0))
    kw_pad, vw_pad = jnp.pad(kw, pad), jnp.pad(vw, pad)
    n_qb = t // Q_BLOCK
    qb = q.reshape(b, n_qb, Q_BLOCK, NSA_KV_HEADS, NSA_GROUP, HEAD_DIM).swapaxes(0, 1)
    gb = gates.reshape(b, n_qb, Q_BLOCK, NSA_HEADS, 3).swapaxes(0, 1)

    def one_block(args):
        i, q_i, g_i = args
        q0 = i * Q_BLOCK
        qpos = q0 + jnp.arange(Q_BLOCK)
        kw_i = lax.dynamic_slice_in_dim(kw_pad, q0, WINDOW + Q_BLOCK, axis=1)
        vw_i = lax.dynamic_slice_in_dim(vw_pad, q0, WINDOW + Q_BLOCK, axis=1)
        kwpos = q0 - WINDOW + jnp.arange(WINDOW + Q_BLOCK)
        return nsa_core(q_i, qpos, kc, vc, cend, n_slc, gather_sel, kw_i, vw_i, kwpos, g_i, rel_table)

    o = lax.map(one_block, (jnp.arange(n_qb), qb, gb))
    return o.swapaxes(0, 1).reshape(b, t, NSA_HEADS * HEAD_DIM)


def nsa_sample(l, q, kc_new, vc_new, ks_new, vs_new, kw_new, vw_new, gates,
               cache_cmp_k, cache_cmp_v, cache_sel_k, cache_sel_v, cache_win_k, cache_win_v,
               page_table, w_phi_l, gk_cmp, rel_table):
    n, t = q.shape[:2]
    past = page_table.shape[1] * PAGE_SIZE

    def with_past(pool, new):
        old = pool[l, page_table].reshape(n, past, NSA_KV_HEADS, HEAD_DIM).astype(new.dtype)
        return jnp.concatenate([old, new], axis=1)

    kc = rms_norm(compress(with_past(cache_cmp_k, kc_new), w_phi_l[0]), gk_cmp)
    vc = compress(with_past(cache_cmp_v, vc_new), w_phi_l[1])
    cend = jnp.arange(kc.shape[1]) * CMP_STRIDE + CMP_BLOCK - 1

    n_past_blk = past // SEL_BLOCK
    n_new_blk = -(-t // SEL_BLOCK)
    blocks_per_page = PAGE_SIZE // SEL_BLOCK
    pad = ((0, 0), (0, n_new_blk * SEL_BLOCK - t), (0, 0), (0, 0))
    ks_pad, vs_pad = jnp.pad(ks_new, pad), jnp.pad(vs_new, pad)
    ni = jnp.arange(n)[:, None, None, None]
    hi = jnp.arange(NSA_KV_HEADS)[None, None, :, None, None]
    off = jnp.arange(SEL_BLOCK)

    def gather_sel(idx):
        jp = jnp.minimum(idx, n_past_blk - 1)
        phys = page_table[ni, jp // blocks_per_page][..., None]
        inrow = ((jp % blocks_per_page) * SEL_BLOCK)[..., None] + off
        nrow = (jnp.clip(idx - n_past_blk, 0, n_new_blk - 1) * SEL_BLOCK)[..., None] + off
        use_past = (idx < n_past_blk)[..., None, None]

        def pick(pool, new_pad):
            return jnp.where(use_past, pool[l, phys, inrow, hi].astype(new_pad.dtype),
                             new_pad[ni[..., None], nrow, hi])

        return pick(cache_sel_k, ks_pad), pick(cache_sel_v, vs_pad)

    kw = jnp.concatenate([cache_win_k[l].astype(kw_new.dtype), kw_new], axis=1)
    vw = jnp.concatenate([cache_win_v[l].astype(vw_new.dtype), vw_new], axis=1)
    wb = cache_win_k.shape[2]
    kwpos = past - wb + jnp.arange(wb + t)
    qpos = past + jnp.arange(t)
    o = nsa_core(q, qpos, kc, vc, cend, n_past_blk + n_new_blk, gather_sel, kw, vw, kwpos, gates, rel_table)
    return o, kw[:, -wb:], vw[:, -wb:]


def mixer_inputs(h, w_in_l, qk_gain_l):
    zq, zf, zi, zg, zq_b, kc, vc, ks, vs, kw, vw, zgate, ga, gb = split_cols(h @ w_in_l)
    n, t = h.shape[:2]

    def kv(z):
        return z.reshape(n, t, NSA_KV_HEADS, HEAD_DIM)

    q = rms_norm(zq_b.reshape(n, t, NSA_KV_HEADS, NSA_GROUP, HEAD_DIM), qk_gain_l[0])
    return ((zq, zf, zi, zg), q, kv(kc), kv(vc), rms_norm(kv(ks), qk_gain_l[2]), kv(vs),
            rms_norm(kv(kw), qk_gain_l[3]), kv(vw), zgate.reshape(n, t, NSA_HEADS, 3), ga, gb)


def merge(o_a, o_b, ga, gb, w_pa, w_pb, w_o):
    return (jax.nn.sigmoid(ga) * (o_a @ w_pa) + jax.nn.sigmoid(gb) * (o_b @ w_pb)) @ w_o


def setup_inputs(seed: int = 0) -> dict:
    key = jax.random.key(seed)
    ks = jax.random.split(key, 26)
    f32 = jnp.float32
    n_pages = PAST_LEN // PAGE_SIZE
    n_phys = (DEC_BATCH * n_pages * 5) // 4
    wb = min(WINDOW, PAST_LEN)

    def nrm(k, shape, scale=1.0):
        return jax.random.normal(k, shape, f32) * scale

    pool = (DEPTH, n_phys, PAGE_SIZE, NSA_KV_HEADS, HEAD_DIM)
    buf = (DEPTH, DEC_BATCH, wb, NSA_KV_HEADS, HEAD_DIM)
    page_table = jax.random.permutation(ks[9], n_phys)[:DEC_BATCH * n_pages].reshape(
        DEC_BATCH, n_pages).astype(jnp.int32)
    return {
        'x_prompt': nrm(ks[0], (BATCH, SEQ, D_MODEL)),
        'x_sample': nrm(ks[1], (DEC_BATCH, DEC_SEQ, D_MODEL)),
        'cache_cmp_k': nrm(ks[2], pool),
        'cache_cmp_v': nrm(ks[3], pool),
        'cache_sel_k': nrm(ks[4], pool),
        'cache_sel_v': nrm(ks[5], pool),
        'cache_win_k': nrm(ks[6], buf),
        'cache_win_v': nrm(ks[7], buf),
        'state_hgrn': nrm(ks[8], (DEPTH, DEC_BATCH, HG_HEADS, HG_DK, HG_DV), 0.3),
        'page_table': page_table,
        'c_prompt': nrm(ks[10], (BATCH, D_MODEL)),
        'c_sample': nrm(ks[11], (DEC_BATCH, D_MODEL)),
        'w_ada': nrm(ks[12], (DEPTH, D_MODEL, 6 * D_MODEL), 0.5 * D_MODEL ** -0.5),
        'b_ada': nrm(ks[13], (DEPTH, 6 * D_MODEL), 0.01),
        'norm_gain': 1.0 + nrm(ks[14], (DEPTH, 2, D_MODEL), 0.02),
        'w_in': nrm(ks[15], (DEPTH, D_MODEL, IN_WIDTH), D_MODEL ** -0.5),
        'lb_param': nrm(ks[16], (DEPTH, HG_HEADS * HG_DK)),
        'hgrn_gain': 1.0 + nrm(ks[17], (DEPTH, HG_DV), 0.02),
        'qk_gain': 1.0 + nrm(ks[18], (DEPTH, 4, HEAD_DIM), 0.02),
        'w_phi': nrm(ks[19], (DEPTH, 2, NSA_KV_HEADS, 2, CMP_STRIDE, HEAD_DIM, HEAD_DIM),
                     (CMP_BLOCK * HEAD_DIM) ** -0.5),
        'rel_bias': nrm(ks[20], (N_BUCKETS, NSA_HEADS), 0.5),
        'w_proj_a': nrm(ks[21], (DEPTH, HG_HEADS * HG_DV, D_MODEL), (HG_HEADS * HG_DV) ** -0.5),
        'w_proj_b': nrm(ks[22], (DEPTH, NSA_HEADS * HEAD_DIM, D_MODEL), (NSA_HEADS * HEAD_DIM) ** -0.5),
        'w_out': nrm(ks[23], (DEPTH, D_MODEL, D_MODEL), D_MODEL ** -0.5),
        'w_ffn_in': nrm(ks[24], (DEPTH, D_MODEL, 2 * D_FF), D_MODEL ** -0.5),
        'w_ffn_out': nrm(ks[25], (DEPTH, D_FF, D_MODEL), D_FF ** -0.5),
    }


def reference(x_prompt, x_sample, cache_cmp_k, cache_cmp_v, cache_sel_k, cache_sel_v,
              cache_win_k, cache_win_v, state_hgrn, page_table, c_prompt, c_sample,
              w_ada, b_ada, norm_gain, w_in, lb_param, hgrn_gain, qk_gain, w_phi, rel_bias,
              w_proj_a, w_proj_b, w_out, w_ffn_in, w_ffn_out):
    f32 = jnp.float32
    sm = jax.nn.softmax(lb_param.astype(f32), axis=0)
    lower = jnp.cumsum(sm, axis=0) - sm[0]

    x = x_prompt
    p_ck, p_cv, p_sk, p_sv, p_wk, p_wv, p_hg = [], [], [], [], [], [], []
    for l in range(DEPTH):
        sh1, sc1, g1, sh2, sc2, g2 = ada(c_prompt, w_ada[l], b_ada[l])
        h = modulate(x, norm_gain[l, 0], sh1, sc1)
        hg, q, kc, vc, ks, vs, kw, vw, gates, ga, gb = mixer_inputs(h, w_in[l], qk_gain[l])
        s0 = jnp.zeros((x.shape[0], HG_HEADS, HG_DK, HG_DV), f32)
        o_a, s_new = hgrn2_branch(*hg, lower[l], hgrn_gain[l], s0)
        o_b = nsa_prompt(q, kc, vc, ks, vs, kw, vw, gates, w_phi[l], qk_gain[l, 1], rel_bias)
        x = x + g1[:, None, :] * merge(o_a, o_b, ga, gb, w_proj_a[l], w_proj_b[l], w_out[l])
        h = modulate(x, norm_gain[l, 1], sh2, sc2)
        x = x + g2[:, None, :] * swiglu(h, w_ffn_in[l], w_ffn_out[l])
        wb = min(WINDOW, x.shape[1])
        p_ck.append(kc)
        p_cv.append(vc)
        p_sk.append(ks)
        p_sv.append(vs)
        p_wk.append(kw[:, -wb:])
        p_wv.append(vw[:, -wb:])
        p_hg.append(s_new)
    y_prompt = x

    x = x_sample
    s_ck, s_cv, s_sk, s_sv, s_wk, s_wv, s_hg = [], [], [], [], [], [], []
    for l in range(DEPTH):
        sh1, sc1, g1, sh2, sc2, g2 = ada(c_sample, w_ada[l], b_ada[l])
        h = modulate(x, norm_gain[l, 0], sh1, sc1)
        hg, q, kc, vc, ks, vs, kw, vw, gates, ga, gb = mixer_inputs(h, w_in[l], qk_gain[l])
        o_a, s_new = hgrn2_branch(*hg, lower[l], hgrn_gain[l], state_hgrn[l])
        o_b, wk_new, wv_new = nsa_sample(l, q, kc, vc, ks, vs, kw, vw, gates,
                                         cache_cmp_k, cache_cmp_v, cache_sel_k, cache_sel_v,
                                         cache_win_k, cache_win_v, page_table,
                                         w_phi[l], qk_gain[l, 1], rel_bias)
        x = x + g1[:, None, :] * merge(o_a, o_b, ga, gb, w_proj_a[l], w_proj_b[l], w_out[l])
        h = modulate(x, norm_gain[l, 1], sh2, sc2)
        x = x + g2[:, None, :] * swiglu(h, w_ffn_in[l], w_ffn_out[l])
        s_ck.append(kc)
        s_cv.append(vc)
        s_sk.append(ks)
        s_sv.append(vs)
        s_wk.append(wk_new)
        s_wv.append(wv_new)
        s_hg.append(s_new)
    y_sample = x

    return (y_prompt, y_sample,
            jnp.stack(p_ck), jnp.stack(p_cv), jnp.stack(p_sk), jnp.stack(p_sv),
            jnp.stack(p_wk), jnp.stack(p_wv), jnp.stack(p_hg),
            jnp.stack(s_ck), jnp.stack(s_cv), jnp.stack(s_sk), jnp.stack(s_sv),
            jnp.stack(s_wk), jnp.stack(s_wv), jnp.stack(s_hg))
```

```python
import functools
import math

import numpy as np
import jax
import jax.numpy as jnp
from jax import lax
from jax.experimental import pallas as pl
from jax.experimental.pallas import tpu as pltpu

F32 = jnp.float32
BF16 = jnp.bfloat16

HEAD = 128
KVH = 4
GRP = 4
NH = KVH * GRP
KVW = KVH * HEAD
HG_CHUNK = 64
HG_SUB = 16
CMP_STRIDE = 16
CMP_BLOCK = 32
SEL_BLOCK = 64
N_SELECT = 16
WINDOW = 512
QB = 128
PAGE = 128
N_BUCKETS = 32
MAX_DISTANCE = 128
EPS = 1e-6
NEG = -1e30
SELECT_BONUS = 1e3
SCALE = HEAD ** -0.5
VMEM_LIMIT = 56 * 1024 * 1024


def _cparams(sem):
    return pltpu.CompilerParams(dimension_semantics=sem, vmem_limit_bytes=VMEM_LIMIT)


def _bucket_np(dist):
    n = np.maximum(np.asarray(dist, np.int64), 0)
    max_exact = N_BUCKETS // 2
    nf = np.maximum(n, 1).astype(np.float32)
    large = max_exact + (np.log(nf / np.float32(max_exact)) / np.float32(math.log(MAX_DISTANCE / max_exact))
                         * np.float32(N_BUCKETS - max_exact)).astype(np.int32)
    return np.where(n < max_exact, n, np.minimum(large, N_BUCKETS - 1)).astype(np.int32)


def _split3(x):
    hi = x.astype(BF16)
    r1 = x - hi.astype(F32)
    mid = r1.astype(BF16)
    lo = (r1 - mid.astype(F32)).astype(BF16)
    return hi, mid, lo


def _dot(a, b):
    return jnp.dot(a, b, preferred_element_type=F32)


def _dot_nt(a, b):
    return lax.dot_general(a, b, (((1,), (1,)), ((), ())), preferred_element_type=F32)


def _dot_tn(a, b):
    return lax.dot_general(a, b, (((0,), (0,)), ((), ())), preferred_element_type=F32)


def _dot_exact_rhs(a01, x):
    hi, mid, lo = _split3(x)
    a = a01.astype(BF16)
    return _dot(a, hi) + _dot(a, mid) + _dot(a, lo)


def _dot_nt_exact_rhs(a01, x):
    hi, mid, lo = _split3(x)
    a = a01.astype(BF16)
    return _dot_nt(a, hi) + _dot_nt(a, mid) + _dot_nt(a, lo)


def _sigmoid(x):
    return 1.0 / (1.0 + jnp.exp(-x))


def _silu(x):
    return x * _sigmoid(x)


def _ada_kernel(c_ref, w_ref, b_ref, o_ref):
    c = c_ref[...]
    o_ref[...] = _dot(_silu(c).astype(BF16), w_ref[...]) + b_ref[...]


def ada_all(c, w_bf, b):
    m, d = c.shape
    n = w_bf.shape[1]
    tn = 512
    return pl.pallas_call(
        _ada_kernel,
        grid=(n // tn,),
        in_specs=[pl.BlockSpec((m, d), lambda j: (0, 0)),
                  pl.BlockSpec((d, tn), lambda j: (0, j)),
                  pl.BlockSpec((1, tn), lambda j: (0, j))],
        out_specs=pl.BlockSpec((m, tn), lambda j: (0, j)),
        out_shape=jax.ShapeDtypeStruct((m, n), F32),
        compiler_params=_cparams(("parallel",)),
    )(c, w_bf, b)


def _group_spec(rows_per_group, tm, tn, col_block=None):
    def imap(i, j):
        return ((i * tm) // rows_per_group, 0, j if col_block is None else col_block)
    return pl.BlockSpec((None, 1, tn), imap)


def _modulate_kernel(x_ref, gain_ref, scale_ref, shift_ref, o_ref):
    x = x_ref[...]
    y = x * lax.rsqrt(jnp.mean(x * x, axis=-1, keepdims=True) + EPS)
    y = y * gain_ref[...]
    o_ref[...] = (y * (1.0 + scale_ref[...]) + shift_ref[...]).astype(o_ref.dtype)


def modulate(x, gain, scale, shift, rows_per_group):
    m, d = x.shape
    tm = min(256, rows_per_group) if rows_per_group > 1 else min(256, m)
    if rows_per_group == 1:
        vec = pl.BlockSpec((tm, d), lambda i: (i, 0))
        scale, shift = scale.reshape(m, d), shift.reshape(m, d)
    else:
        vec = pl.BlockSpec((None, 1, d), lambda i: ((i * tm) // rows_per_group, 0, 0))
    return pl.pallas_call(
        _modulate_kernel,
        grid=(m // tm,),
        in_specs=[pl.BlockSpec((tm, d), lambda i: (i, 0)),
                  pl.BlockSpec((1, d), lambda i: (0, 0)), vec, vec],
        out_specs=pl.BlockSpec((tm, d), lambda i: (i, 0)),
        out_shape=jax.ShapeDtypeStruct((m, d), BF16),
        compiler_params=_cparams(("parallel",)),
    )(x, gain, scale, shift)


def _mm_kernel(a_ref, b_ref, o_ref):
    o_ref[...] = _dot(a_ref[...], b_ref[...]).astype(o_ref.dtype)


def _mm_headnorm_kernel(a_ref, b_ref, g_ref, o_ref):
    z = _dot(a_ref[...], b_ref[...])
    tm, tn = z.shape
    for h in range(tn // HEAD):
        zh = z[:, h * HEAD:(h + 1) * HEAD]
        y = zh * lax.rsqrt(jnp.mean(zh * zh, axis=-1, keepdims=True) + EPS)
        o_ref[:, h * HEAD:(h + 1) * HEAD] = (y * g_ref[...]).astype(o_ref.dtype)


def _pick_tm(m):
    for tm in (1024, 512, 256, 128):
        if m % tm == 0:
            return tm
    return m


def mm(a, b, out_dtype=F32, head_gain=None, tn=512):
    m, k = a.shape
    n = b.shape[1]
    tn = min(tn, n)
    tm = _pick_tm(m)
    in_specs = [pl.BlockSpec((tm, k), lambda i, j: (i, 0)),
                pl.BlockSpec((k, tn), lambda i, j: (0, j))]
    args = [a, b]
    kern = _mm_kernel
    if head_gain is not None:
        in_specs.append(pl.BlockSpec((1, HEAD), lambda i, j: (0, 0)))
        args.append(head_gain)
        kern = _mm_headnorm_kernel
    return pl.pallas_call(
        kern,
        grid=(m // tm, n // tn),
        in_specs=in_specs,
        out_specs=pl.BlockSpec((tm, tn), lambda i, j: (i, j)),
        out_shape=jax.ShapeDtypeStruct((m, n), out_dtype),
        compiler_params=_cparams(("parallel", "parallel")),
    )(*args)


def _merge_kernel(oa_ref, ob_ref, wa_ref, wb_ref, ga_ref, gb_ref, o_ref):
    ya = _dot(oa_ref[...], wa_ref[...])
    yb = _dot(ob_ref[...], wb_ref[...])
    o_ref[...] = (_sigmoid(ga_ref[...]) * ya + _sigmoid(gb_ref[...]) * yb).astype(o_ref.dtype)


def merge_gate(o_a, o_b, w_pa, w_pb, gab):
    m, k = o_a.shape
    n = w_pa.shape[1]
    tn = 512
    tm = _pick_tm(m)
    nb = n // tn
    return pl.pallas_call(
        _merge_kernel,
        grid=(m // tm, nb),
        in_specs=[pl.BlockSpec((tm, k), lambda i, j: (i, 0)),
                  pl.BlockSpec((tm, k), lambda i, j: (i, 0)),
                  pl.BlockSpec((k, tn), lambda i, j: (0, j)),
                  pl.BlockSpec((k, tn), lambda i, j: (0, j)),
                  pl.BlockSpec((tm, tn), lambda i, j: (i, j)),
                  pl.BlockSpec((tm, tn), lambda i, j: (i, j + nb))],
        out_specs=pl.BlockSpec((tm, tn), lambda i, j: (i, j)),
        out_shape=jax.ShapeDtypeStruct((m, n), BF16),
        compiler_params=_cparams(("parallel", "parallel")),
    )(o_a, o_b, w_pa, w_pb, gab, gab)


def _mm_res_kernel(a_ref, b_ref, x_ref, g_ref, o_ref):
    o_ref[...] = x_ref[...] + g_ref[...] * _dot(a_ref[...], b_ref[...])


def mm_residual(a, b, x, g, rows_per_group):
    m, k = a.shape
    n = b.shape[1]
    tn = 512
    tm = min(_pick_tm(m), rows_per_group) if rows_per_group > 1 else _pick_tm(m)
    if rows_per_group == 1:
        gspec = pl.BlockSpec((tm, tn), lambda i, j: (i, j))
        g = g.reshape(m, n)
    else:
        gspec = _group_spec(rows_per_group, tm, tn)
    return pl.pallas_call(
        _mm_res_kernel,
        grid=(m // tm, n // tn),
        in_specs=[pl.BlockSpec((tm, k), lambda i, j: (i, 0)),
                  pl.BlockSpec((k, tn), lambda i, j: (0, j)),
                  pl.BlockSpec((tm, tn), lambda i, j: (i, j)),
                  gspec],
        out_specs=pl.BlockSpec((tm, tn), lambda i, j: (i, j)),
        out_shape=jax.ShapeDtypeStruct((m, n), F32),
        compiler_params=_cparams(("parallel", "parallel")),
    )(a, b, x, g)


def _ffn_in_kernel(a_ref, bg_ref, bu_ref, o_ref):
    a = a_ref[...]
    gate = _dot(a, bg_ref[...])
    up = _dot(a, bu_ref[...])
    o_ref[...] = (_silu(gate) * up).astype(o_ref.dtype)


def ffn_in(a, w1):
    m, k = a.shape
    f = w1.shape[1] // 2
    tn = 512
    nb = f // tn
    tm = _pick_tm(m)
    return pl.pallas_call(
        _ffn_in_kernel,
        grid=(m // tm, nb),
        in_specs=[pl.BlockSpec((tm, k), lambda i, j: (i, 0)),
                  pl.BlockSpec((k, tn), lambda i, j: (0, j)),
                  pl.BlockSpec((k, tn), lambda i, j: (0, j + nb))],
        out_specs=pl.BlockSpec((tm, tn), lambda i, j: (i, j)),
        out_shape=jax.ShapeDtypeStruct((m, f), BF16),
        compiler_params=_cparams(("parallel", "parallel")),
    )(a, w1, w1)


def _hgrn_out(o, gain, zg):
    y = o * lax.rsqrt(jnp.mean(o * o, axis=-1, keepdims=True) + EPS)
    return y * gain * _silu(zg)


def _hgrn_prompt_kernel(zq_ref, zf_ref, zi_ref, zg_ref, lower_ref, gain_ref, o_ref, s_ref, state, *, tt):
    t = pl.program_id(2)

    @pl.when(t == 0)
    def _():
        state[...] = jnp.zeros_like(state)

    lower = lower_ref[...]
    gain = gain_ref[...]
    c = HG_CHUNK
    tril = (lax.broadcasted_iota(jnp.int32, (c, c), 0) >= lax.broadcasted_iota(jnp.int32, (c, c), 1)).astype(F32)
    tio = lax.broadcasted_iota(jnp.int32, (HG_SUB, 1), 0)

    def chunk(ci, carry):
        r0 = pl.multiple_of(ci * c, c)
        q = zq_ref[pl.ds(r0, c), :]
        f = lower + (1.0 - lower) * _sigmoid(zf_ref[pl.ds(r0, c), :])
        k = 1.0 - f
        v = zi_ref[pl.ds(r0, c), :]
        b = _dot_exact_rhs(tril, jnp.log(f))
        s_old = state[...]
        o_inter = _dot((q * jnp.exp(b)).astype(BF16), s_old.astype(BF16))
        blast = b[c - 1:c, :]
        kd = k * jnp.exp(blast - b)
        decay_col = jnp.transpose(jnp.broadcast_to(jnp.exp(blast), (HEAD, HEAD)))
        state[...] = decay_col * s_old + _dot_tn(kd.astype(BF16), v.astype(BF16))
        v_bf = v.astype(BF16)
        for i in range(c // HG_SUB):
            lo = i * HG_SUB
            qi, ki, vi, bi = (a[lo:lo + HG_SUB, :] for a in (q, k, v, b))
            acc = o_inter[lo:lo + HG_SUB, :]
            if i > 0:
                r = b[lo - 1:lo, :]
                qd = qi * jnp.exp(bi - r)
                kp = k[:lo, :] * jnp.exp(r - b[:lo, :])
                a_mat = _dot_nt(qd.astype(BF16), kp.astype(BF16))
                acc = acc + _dot(a_mat.astype(BF16), v_bf[:lo, :])
            for s in range(HG_SUB):
                e = jnp.exp(jnp.where(tio >= s, bi - bi[s:s + 1, :], NEG))
                w = jnp.sum(qi * ki[s:s + 1, :] * e, axis=-1, keepdims=True)
                acc = acc + w * vi[s:s + 1, :]
            zg = zg_ref[pl.ds(pl.multiple_of(r0 + lo, HG_SUB), HG_SUB), :]
            o_ref[pl.ds(pl.multiple_of(r0 + lo, HG_SUB), HG_SUB), :] = _hgrn_out(acc, gain, zg).astype(o_ref.dtype)
        return carry

    lax.fori_loop(0, tt // c, chunk, 0)

    @pl.when(t == pl.num_programs(2) - 1)
    def _():
        s_ref[...] = state[...]


def hgrn_prompt(zh, lower, gain, nb, t_len):
    m = zh.shape[0]
    nh = zh.shape[1] // (4 * HEAD)
    tt = min(512, t_len)
    ntt = t_len // tt

    def spec(part):
        return pl.BlockSpec((tt, HEAD), lambda b, h, t: (b * ntt + t, part * nh + h))

    return pl.pallas_call(
        functools.partial(_hgrn_prompt_kernel, tt=tt),
        grid=(nb, nh, ntt),
        in_specs=[spec(0), spec(1), spec(2), spec(3),
                  pl.BlockSpec((1, HEAD), lambda b, h, t: (0, h)),
                  pl.BlockSpec((1, HEAD), lambda b, h, t: (0, 0))],
        out_specs=[pl.BlockSpec((tt, HEAD), lambda b, h, t: (b * ntt + t, h)),
                   pl.BlockSpec((None, None, HEAD, HEAD), lambda b, h, t: (b, h, 0, 0))],
        out_shape=[jax.ShapeDtypeStruct((m, nh * HEAD), BF16),
                   jax.ShapeDtypeStruct((nb, nh, HEAD, HEAD), F32)],
        scratch_shapes=[pltpu.VMEM((HEAD, HEAD), F32)],
        compiler_params=_cparams(("parallel", "parallel", "arbitrary")),
    )(zh, zh, zh, zh, lower, gain)


def _hgrn_sample_kernel(zqt_ref, zft_ref, zi_ref, zg_ref, lowt_ref, gain_ref, s_in_ref, o_ref, s_out_ref, o_acc):
    n = s_in_ref.shape[0]
    f_t = lowt_ref[...] + (1.0 - lowt_ref[...]) * _sigmoid(zft_ref[...])
    qf_t = zqt_ref[...]
    k_t = 1.0 - f_t
    for s in range(n):
        s_new = f_t[:, s:s + 1] * s_in_ref[s] + k_t[:, s:s + 1] * zi_ref[s:s + 1, :]
        s_out_ref[s] = s_new
        o_acc[s:s + 1, :] = jnp.sum(qf_t[:, s:s + 1] * s_new, axis=0, keepdims=True)
    o_ref[...] = _hgrn_out(o_acc[...], gain_ref[...], zg_ref[...]).astype(o_ref.dtype)


def hgrn_sample(zh, lower, gain, state_l):
    n = zh.shape[0]
    nh = state_l.shape[1]
    d = nh * HEAD
    zq_t = zh[:, :d].reshape(n, nh, HEAD).transpose(1, 2, 0)
    zf_t = zh[:, d:2 * d].reshape(n, nh, HEAD).transpose(1, 2, 0)
    low_t = lower.reshape(nh, HEAD, 1)
    tspec = pl.BlockSpec((None, HEAD, n), lambda h: (h, 0, 0))
    return pl.pallas_call(
        _hgrn_sample_kernel,
        grid=(nh,),
        in_specs=[tspec, tspec,
                  pl.BlockSpec((n, HEAD), lambda h: (0, 2 * nh + h)),
                  pl.BlockSpec((n, HEAD), lambda h: (0, 3 * nh + h)),
                  pl.BlockSpec((None, HEAD, 1), lambda h: (h, 0, 0)),
                  pl.BlockSpec((1, HEAD), lambda h: (0, 0)),
                  pl.BlockSpec((n, None, HEAD, HEAD), lambda h: (0, h, 0, 0))],
        out_specs=[pl.BlockSpec((n, HEAD), lambda h: (0, h)),
                   pl.BlockSpec((n, None, HEAD, HEAD), lambda h: (0, h, 0, 0))],
        out_shape=[jax.ShapeDtypeStruct((n, d), BF16),
                   jax.ShapeDtypeStruct(state_l.shape, F32)],
        scratch_shapes=[pltpu.VMEM((n, HEAD), F32)],
        compiler_params=_cparams(("parallel",)),
    )(zq_t, zf_t, zh, zh, low_t, gain, state_l)


def _gather_kernel(pt_ref, pool_ref, o_ref, sem, *, n_pages, layer):
    n = pl.program_id(0)

    def page_copy(pg):
        return pltpu.make_async_copy(pool_ref.at[layer, pt_ref[n, pg]], o_ref.at[n, pg], sem.at[pg])

    for pg in range(n_pages):
        page_copy(pg).start()
    for pg in range(n_pages):
        page_copy(pg).wait()


def gather_pages(pool, page_table, layer):
    n, n_pages = page_table.shape
    return pl.pallas_call(
        functools.partial(_gather_kernel, n_pages=n_pages, layer=layer),
        grid_spec=pltpu.PrefetchScalarGridSpec(
            num_scalar_prefetch=1,
            grid=(n,),
            in_specs=[pl.BlockSpec(memory_space=pl.ANY)],
            out_specs=pl.BlockSpec(memory_space=pl.ANY),
            scratch_shapes=[pltpu.SemaphoreType.DMA((n_pages,))]),
        out_shape=jax.ShapeDtypeStruct((n, n_pages) + pool.shape[2:], pool.dtype),
        compiler_params=_cparams(("arbitrary",)),
    )(page_table, pool)


def _tail_kernel(x_ref, w_ref, o_ref):
    for k in range(KVH):
        o_ref[:, k * HEAD:(k + 1) * HEAD] = _dot(x_ref[:, k * HEAD:(k + 1) * HEAD].astype(BF16), w_ref[k, 1, 0])


def tail_proj(new_rows, w):
    n = new_rows.shape[0]
    return pl.pallas_call(
        _tail_kernel,
        out_shape=jax.ShapeDtypeStruct((n, KVW), F32),
        compiler_params=pltpu.CompilerParams(vmem_limit_bytes=VMEM_LIMIT),
    )(new_rows, w)


def _compress_kernel(*refs, rows, norm, has_tail):
    if has_tail:
        x_ref, w_ref, gain_ref, tail_ref, o_ref = refs
    else:
        x_ref, w_ref, gain_ref, o_ref = refs
    sb = x_ref.shape[0]
    m = sb * rows
    row = lax.broadcasted_iota(jnp.int32, (m, 1), 0)
    for k in range(KVH):
        u0 = jnp.zeros((m, HEAD), F32)
        u1 = jnp.zeros((m, HEAD), F32)
        for p in range(CMP_STRIDE):
            c0 = p * KVW + k * HEAD
            xs = x_ref[:, :, c0:c0 + HEAD].reshape(m, HEAD).astype(BF16)
            u0 = u0 + _dot(xs, w_ref[k, 0, p])
            u1 = u1 + _dot(xs, w_ref[k, 1, p])
        nxt = pltpu.roll(u1, m - 1, axis=0)
        for s in range(sb):
            last = row == s * rows + rows - 1
            if has_tail:
                nxt = jnp.where(last, tail_ref[s, :, k * HEAD:(k + 1) * HEAD], nxt)
            else:
                nxt = jnp.where(last, 0.0, nxt)
                u0 = jnp.where(last, 0.0, u0)
        out = u0 + nxt
        if norm:
            out = out * lax.rsqrt(jnp.mean(out * out, axis=-1, keepdims=True) + EPS) * gain_ref[...]
        o_ref[:, :, k * HEAD:(k + 1) * HEAD] = out.reshape(sb, rows, HEAD)


def compress(x, w, gain, tail=None, norm=False, sb=1):
    s, rows, width = x.shape
    in_specs = [pl.BlockSpec((sb, rows, width), lambda i: (i, 0, 0)),
                pl.BlockSpec(w.shape, lambda i: (0,) * w.ndim),
                pl.BlockSpec((1, HEAD), lambda i: (0, 0))]
    args = [x, w, gain]
    if tail is not None:
        in_specs.append(pl.BlockSpec((sb, 1, KVW), lambda i: (i, 0, 0)))
        args.append(tail)
    return pl.pallas_call(
        functools.partial(_compress_kernel, rows=rows, norm=norm, has_tail=tail is not None),
        grid=(s // sb,),
        in_specs=in_specs,
        out_specs=pl.BlockSpec((sb, rows, KVW), lambda i: (i, 0, 0)),
        out_shape=jax.ShapeDtypeStruct((s, rows, KVW), F32),
        compiler_params=_cparams(("parallel",)),
    )(*args)


def _overlap(n_slc, ncp):
    c_start = lax.broadcasted_iota(jnp.int32, (n_slc, ncp), 1) * CMP_STRIDE
    s_start = lax.broadcasted_iota(jnp.int32, (n_slc, ncp), 0) * SEL_BLOCK
    return jnp.where(c_start < s_start + SEL_BLOCK, jnp.where(c_start + CMP_BLOCK > s_start, 1.0, 0.0), 0.0)


def _masked_softmax(s, valid):
    s = jnp.where(valid, s, NEG)
    m = jnp.max(s, axis=-1, keepdims=True)
    p = jnp.where(valid, jnp.exp(s - m), 0.0)
    return p / jnp.maximum(jnp.sum(p, axis=-1, keepdims=True), 1e-30)


NEAR_W = 32
NEAR_BACK = 16


def _dot_exact_lhs(x, b01):
    hi, mid, lo = _split3(x)
    return _dot(hi, b01) + _dot(mid, b01) + _dot(lo, b01)


def _nsa_cmp_prompt_kernel(q_ref, kc_ref, vc_ref, near_ref, far_ref, o_ref, imp_ref, *, n_slc):
    i = pl.program_id(2)
    ncp = kc_ref.shape[0]
    kc = kc_ref[...].astype(BF16)
    vc = vc_ref[...].astype(BF16)
    qpos = i * QB + lax.broadcasted_iota(jnp.int32, (QB, ncp), 0)
    cend = lax.broadcasted_iota(jnp.int32, (QB, ncp), 1) * CMP_STRIDE + (CMP_BLOCK - 1)
    valid = qpos >= cend
    rel = lax.broadcasted_iota(jnp.int32, (NEAR_W, ncp), 1) - i * (QB // CMP_STRIDE) + NEAR_BACK
    place = jnp.where(rel == lax.broadcasted_iota(jnp.int32, (NEAR_W, ncp), 0), 1.0, 0.0).astype(BF16)
    psum = jnp.zeros((QB, ncp), F32)
    for g in range(GRP):
        bias = far_ref[g] + _dot_exact_lhs(near_ref[g], place)
        s = _dot_nt(q_ref[:, g * HEAD:(g + 1) * HEAD], kc) * SCALE + bias
        p = _masked_softmax(s, valid)
        o_ref[:, g * HEAD:(g + 1) * HEAD] = _dot(p.astype(BF16), vc)
        psum = psum + p
    imp_ref[...] = _dot_nt_exact_rhs(_overlap(n_slc, ncp), psum)


def nsa_cmp_prompt(q, kcc, vcc, near, far, nb, t_len):
    n_qb = t_len // QB
    ncp = kcc.shape[1]
    n_slc = t_len // SEL_BLOCK
    return pl.pallas_call(
        functools.partial(_nsa_cmp_prompt_kernel, n_slc=n_slc),
        grid=(nb, KVH, n_qb),
        in_specs=[pl.BlockSpec((QB, GRP * HEAD), lambda b, k, i: (b * n_qb + i, k)),
                  pl.BlockSpec((None, ncp, HEAD), lambda b, k, i: (b, 0, k)),
                  pl.BlockSpec((None, ncp, HEAD), lambda b, k, i: (b, 0, k)),
                  pl.BlockSpec((GRP, QB, NEAR_W), lambda b, k, i: (k, 0, 0)),
                  pl.BlockSpec((GRP, 1, ncp), lambda b, k, i: (k, 0, 0))],
        out_specs=[pl.BlockSpec((QB, GRP * HEAD), lambda b, k, i: (b * n_qb + i, k)),
                   pl.BlockSpec((None, None, n_slc, QB), lambda b, k, i: (b, k, 0, i))],
        out_shape=[jax.ShapeDtypeStruct((nb * t_len, NH * HEAD), F32),
                   jax.ShapeDtypeStruct((nb, KVH, n_slc, t_len), F32)],
        compiler_params=_cparams(("parallel", "parallel", "parallel")),
    )(q, kcc, vcc, near, far)


def _topk_kernel(imp_ref, qpos_ref, o_ref, *, n_slc):
    imp = imp_ref[0]
    for g in range(1, imp_ref.shape[0]):
        imp = imp + imp_ref[g]
    blk = lax.broadcasted_iota(jnp.int32, imp.shape, 0)
    qpos = qpos_ref[...]
    cur = lax.shift_right_logical(qpos, 6)
    forced = jnp.where(blk == 0, 1.0, jnp.where(blk == cur, 1.0, jnp.where(blk == cur - 1, 1.0, 0.0)))
    score = jnp.where(blk * SEL_BLOCK <= qpos, imp + SELECT_BONUS * forced, NEG)
    rank = jnp.zeros(imp.shape, F32)
    for s2 in range(n_slc):
        other = score[s2:s2 + 1, :]
        rank = rank + jnp.where(other > score, 1.0, jnp.where(other == score, jnp.where(blk > s2, 1.0, 0.0), 0.0))
    keep = jnp.where(rank < float(min(N_SELECT, n_slc)), jnp.where(blk < n_slc, 1.0, 0.0), 0.0)
    o_ref[...] = keep


def topk_mask(imp, qpos, n_slc):
    r, g, s, q = imp.shape
    tq = _pick_tm(q) if q % QB == 0 else q
    tq = min(tq, 512)
    return pl.pallas_call(
        functools.partial(_topk_kernel, n_slc=n_slc),
        grid=(r, q // tq),
        in_specs=[pl.BlockSpec((None, g, s, tq), lambda a, t: (a, 0, 0, t)),
                  pl.BlockSpec((1, tq), lambda a, t: (0, t))],
        out_specs=pl.BlockSpec((None, s, tq), lambda a, t: (a, 0, t)),
        out_shape=jax.ShapeDtypeStruct((r, s, q), F32),
        compiler_params=_cparams(("parallel", "parallel")),
    )(imp, qpos)


def _nsa_selwin_prompt_kernel(q_ref, ks_ref, vs_ref, kw_ref, vw_ref, mask_ref, bias_ref, ocmp_ref, gate_ref,
                              o_ref, m_s, l_s, acc_s, *, n_slc):
    i = pl.program_id(2)
    rows = GRP * QB
    q = jnp.concatenate([q_ref[:, g * HEAD:(g + 1) * HEAD] for g in range(GRP)], axis=0)
    mask_bf = mask_ref[...].astype(BF16)
    blk_row = lax.broadcasted_iota(jnp.int32, (n_slc, QB), 0)
    blk_of_key = lax.shift_right_logical(lax.broadcasted_iota(jnp.int32, (n_slc, QB), 1), 6)

    def attend(k_ref, v_ref, kt_lo, selected):
        m_s[...] = jnp.full(m_s.shape, -jnp.inf, F32)
        l_s[...] = jnp.zeros(l_s.shape, F32)
        acc_s[...] = jnp.zeros(acc_s.shape, F32)

        def body(kt, carry):
            k0 = pl.multiple_of(kt * QB, QB)
            kk = k_ref[pl.ds(k0, QB), :].astype(BF16)
            vv = v_ref[pl.ds(k0, QB), :].astype(BF16)
            delta = i - kt
            bias = bias_ref[jnp.minimum(delta, 2) if selected else delta]
            s = (_dot_nt(q, kk) * SCALE).reshape(GRP, QB, QB) + bias
            if selected:
                expand = jnp.where(blk_row == blk_of_key + 2 * kt, 1.0, 0.0).astype(BF16)
                chosen = _dot(mask_bf, expand)
                s = s + ((chosen - 1.0) * 1e30)[None]
            s = s.reshape(rows, QB)
            m_old = m_s[...]
            m_new = jnp.maximum(m_old, jnp.max(s, axis=-1, keepdims=True))
            alpha = jnp.exp(m_old - m_new)
            p = jnp.exp(s - m_new)
            l_s[...] = alpha * l_s[...] + jnp.sum(p, axis=-1, keepdims=True)
            acc_s[...] = alpha * acc_s[...] + _dot(p.astype(BF16), vv)
            m_s[...] = m_new
            return carry

        lax.fori_loop(kt_lo, i + 1, body, 0)
        return acc_s[...] / l_s[...]

    o_sel = attend(ks_ref, vs_ref, 0, True)
    o_win = attend(kw_ref, vw_ref, jnp.maximum(i - WINDOW // QB, 0), False)
    gate = _sigmoid(gate_ref[...])
    for g in range(GRP):
        rs = slice(g * QB, (g + 1) * QB)
        cs = slice(g * HEAD, (g + 1) * HEAD)
        o = (gate[:, 3 * g:3 * g + 1] * ocmp_ref[:, cs] + gate[:, 3 * g + 1:3 * g + 2] * o_sel[rs, :]
             + gate[:, 3 * g + 2:3 * g + 3] * o_win[rs, :])
        o_ref[:, cs] = o.astype(o_ref.dtype)


def nsa_selwin_prompt(q, ks, vs, kw, vw, mask, bias5, o_cmp, gates, nb, t_len):
    n_qb = t_len // QB
    n_slc = t_len // SEL_BLOCK
    rows = GRP * QB
    kv_spec = pl.BlockSpec((None, t_len, HEAD), lambda b, k, i: (b, 0, k))
    q_spec = pl.BlockSpec((QB, GRP * HEAD), lambda b, k, i: (b * n_qb + i, k))
    return pl.pallas_call(
        functools.partial(_nsa_selwin_prompt_kernel, n_slc=n_slc),
        grid=(nb, KVH, n_qb),
        in_specs=[q_spec, kv_spec, kv_spec, kv_spec, kv_spec,
                  pl.BlockSpec((None, None, QB, n_slc), lambda b, k, i: (b, k, i, 0)),
                  pl.BlockSpec((5, GRP, QB, QB), lambda b, k, i: (0, k, 0, 0)),
                  q_spec,
                  pl.BlockSpec((QB, HEAD), lambda b, k, i: (b * n_qb + i, k))],
        out_specs=q_spec,
        out_shape=jax.ShapeDtypeStruct((nb * t_len, NH * HEAD), BF16),
        scratch_shapes=[pltpu.VMEM((rows, 1), F32), pltpu.VMEM((rows, 1), F32), pltpu.VMEM((rows, HEAD), F32)],
        compiler_params=_cparams(("parallel", "parallel", "arbitrary")),
    )(q, ks, vs, kw, vw, mask, bias5, o_cmp, gates)


def _expand_q(q16):
    tiled = jnp.concatenate([q16] * KVH, axis=1)
    head_kv = lax.shift_right_logical(lax.broadcasted_iota(jnp.int32, tiled.shape, 0), 2)
    lane_kv = lax.shift_right_logical(lax.broadcasted_iota(jnp.int32, tiled.shape, 1), 7)
    return jnp.where(head_kv == lane_kv, tiled, 0.0)


def _own_kv_block(o16):
    head_kv = lax.shift_right_logical(lax.broadcasted_iota(jnp.int32, (NH, HEAD), 0), 2)
    out = jnp.zeros((NH, HEAD), F32)
    for k in range(KVH):
        out = out + jnp.where(head_kv == k, o16[:, k * HEAD:(k + 1) * HEAD], 0.0)
    return out


def _nsa_cmp_sample_kernel(q_ref, kc_ref, vc_ref, bias_ref, o_ref, imp_ref):
    sb = q_ref.shape[0]
    ncp = kc_ref.shape[1]
    s_pad = imp_ref.shape[2]
    bias = bias_ref[...]
    valid = bias > 0.5 * NEG
    c_start = lax.broadcasted_iota(jnp.int32, (ncp, s_pad), 0) * CMP_STRIDE
    s_start = lax.broadcasted_iota(jnp.int32, (ncp, s_pad), 1) * SEL_BLOCK
    ovl = jnp.where(c_start < s_start + SEL_BLOCK, jnp.where(c_start + CMP_BLOCK > s_start, 1.0, 0.0), 0.0)
    ovl = ovl.astype(BF16)
    for s in range(sb):
        qx = _expand_q(q_ref[s]).astype(BF16)
        sc = _dot_nt(qx, kc_ref[s].astype(BF16)) * SCALE + bias
        p = _masked_softmax(sc, valid)
        o_ref[s] = _own_kv_block(_dot(p.astype(BF16), vc_ref[s].astype(BF16)))
        hi, mid, lo = _split3(p)
        imp_ref[s] = _dot(hi, ovl) + _dot(mid, ovl) + _dot(lo, ovl)


def nsa_cmp_sample(q, kcc, vcc, bias_cs, s_pad, sb=8):
    n, ncp = kcc.shape[:2]
    return pl.pallas_call(
        _nsa_cmp_sample_kernel,
        grid=(n // sb,),
        in_specs=[pl.BlockSpec((sb, NH, HEAD), lambda i: (i, 0, 0)),
                  pl.BlockSpec((sb, ncp, KVW), lambda i: (i, 0, 0)),
                  pl.BlockSpec((sb, ncp, KVW), lambda i: (i, 0, 0)),
                  pl.BlockSpec((NH, ncp), lambda i: (0, 0))],
        out_specs=[pl.BlockSpec((sb, NH, HEAD), lambda i: (i, 0, 0)),
                   pl.BlockSpec((sb, NH, s_pad), lambda i: (i, 0, 0))],
        out_shape=[jax.ShapeDtypeStruct((n, NH, HEAD), F32),
                   jax.ShapeDtypeStruct((n, NH, s_pad), F32)],
        compiler_params=_cparams(("parallel",)),
    )(q, kcc, vcc, bias_cs)


def _nsa_selwin_sample_kernel(q_ref, ks_ref, vs_ref, ksn_ref, vsn_ref, mask_ref, bs_ref,
                              kw_ref, vw_ref, kwn_ref, vwn_ref, bw_ref, b0_ref, ocmp_ref, gate_ref,
                              o_ref, kwo_ref, vwo_ref, *, n_past_blk):
    p_len = ks_ref.shape[0]
    w_len = kw_ref.shape[0]
    s_pad = mask_ref.shape[1]
    qx32 = _expand_q(q_ref[...])
    qx = qx32.astype(BF16)
    b0 = b0_ref[...]

    def attend(k_ref, v_ref, kn_ref, vn_ref, extra, new_extra):
        s = _dot_nt(qx, k_ref[...].astype(BF16)) * SCALE + extra
        s_new = jnp.sum(qx32 * kn_ref[...], axis=-1, keepdims=True) * SCALE + b0 + new_extra
        m = jnp.maximum(jnp.max(s, axis=-1, keepdims=True), s_new)
        p = jnp.exp(s - m)
        p_new = jnp.exp(s_new - m)
        denom = jnp.sum(p, axis=-1, keepdims=True) + p_new
        o16 = (_dot(p.astype(BF16), v_ref[...].astype(BF16)) + p_new * vn_ref[...]) / denom
        return _own_kv_block(o16)

    blk_row = lax.broadcasted_iota(jnp.int32, (s_pad, p_len), 0)
    blk_of_key = lax.shift_right_logical(lax.broadcasted_iota(jnp.int32, (s_pad, p_len), 1), 6)
    expand = jnp.where(blk_row == blk_of_key, 1.0, 0.0).astype(BF16)
    mask = mask_ref[...]
    chosen = _dot(mask.astype(BF16), expand)
    new_chosen = mask[:, n_past_blk:n_past_blk + 1]
    o_sel = attend(ks_ref, vs_ref, ksn_ref, vsn_ref, bs_ref[...] + (chosen - 1.0) * 1e30, (new_chosen - 1.0) * 1e30)
    o_win = attend(kw_ref, vw_ref, kwn_ref, vwn_ref, bw_ref[...], 0.0)
    gate = _sigmoid(gate_ref[...])
    o = gate[:, 0:1] * ocmp_ref[...] + gate[:, 1:2] * o_sel + gate[:, 2:3] * o_win
    o_ref[...] = o.astype(o_ref.dtype)
    last = lax.broadcasted_iota(jnp.int32, (w_len, 1), 0) == w_len - 1
    kwo_ref[...] = jnp.where(last, kwn_ref[...], pltpu.roll(kw_ref[...], w_len - 1, axis=0))
    vwo_ref[...] = jnp.where(last, vwn_ref[...], pltpu.roll(vw_ref[...], w_len - 1, axis=0))


def nsa_selwin_sample(q, ks, vs, ks_new, vs_new, mask, bias_ss, kw, vw, kw_new, vw_new, bias_ws, bias0,
                      o_cmp, gates, n_past_blk):
    n, p_len = ks.shape[:2]
    w_len = kw.shape[1]
    s_pad = mask.shape[2]

    def per_seq(*shape):
        return pl.BlockSpec((None,) + shape, lambda i: (i,) + (0,) * len(shape))

    def shared(*shape):
        return pl.BlockSpec(shape, lambda i: (0,) * len(shape))

    return pl.pallas_call(
        functools.partial(_nsa_selwin_sample_kernel, n_past_blk=n_past_blk),
        grid=(n,),
        in_specs=[per_seq(NH, HEAD), per_seq(p_len, KVW), per_seq(p_len, KVW), per_seq(1, KVW), per_seq(1, KVW),
                  per_seq(NH, s_pad), shared(NH, p_len),
                  per_seq(w_len, KVW), per_seq(w_len, KVW), per_seq(1, KVW), per_seq(1, KVW),
                  shared(NH, w_len), shared(NH, 1), per_seq(NH, HEAD), per_seq(NH, HEAD)],
        out_specs=[per_seq(NH, HEAD), per_seq(w_len, KVW), per_seq(w_len, KVW)],
        out_shape=[jax.ShapeDtypeStruct((n, NH, HEAD), BF16),
                   jax.ShapeDtypeStruct((n, w_len, KVW), F32),
                   jax.ShapeDtypeStruct((n, w_len, KVW), F32)],
        compiler_params=_cparams(("parallel",)),
    )(q, ks, vs, ks_new, vs_new, mask, bias_ss, kw, vw, kw_new, vw_new, bias_ws, bias0, o_cmp, gates)


def _lookup(table_t, dist, valid=None):
    out = table_t[:, _bucket_np(dist)]
    if valid is not None:
        out = jnp.where(jnp.asarray(valid)[None], out, NEG)
    return out


def _prompt_bias(table_t):
    qi = np.arange(QB)[:, None]
    d_near = qi - (np.arange(NEAR_W)[None, :] - NEAR_BACK) * CMP_STRIDE - (CMP_BLOCK - 1)
    sat = table_t[:, N_BUCKETS - 1]
    near = _lookup(table_t, d_near) - sat[:, None, None]
    near = jnp.where(jnp.asarray(d_near >= 0)[None], near, 0.0)
    kj = np.arange(QB)[None, :]
    d0 = qi - kj
    const = jnp.broadcast_to(sat[:, None, None], (NH, QB, QB))
    tiles = jnp.stack([
        _lookup(table_t, d0, d0 >= 0),
        _lookup(table_t, d0 + QB),
        const, const,
        jnp.where(jnp.asarray(d0 + WINDOW < WINDOW)[None], const, NEG),
    ])
    return near, sat, tiles


def _sample_bias(table_t, past, w_len):
    ncp = past // CMP_STRIDE
    d_c = past - (np.arange(ncp) * CMP_STRIDE + CMP_BLOCK - 1)
    bias_cs = _lookup(table_t, d_c, d_c >= 0)
    bias_ss = _lookup(table_t, past - np.arange(past))
    d_w = w_len - np.arange(w_len)
    bias_ws = _lookup(table_t, d_w, d_w < WINDOW)
    return bias_cs, bias_ss, bias_ws, table_t[:, 0:1]


def _split_w_in(w, d):
    wb = w.astype(BF16)
    o = 4 * d
    parts = {"zh": wb[:, :o], "q": wb[:, o:o + d]}
    o += d
    for name in ("kc", "vc", "ks", "vs", "kw", "vw"):
        parts[name] = wb[:, o:o + KVW]
        o += KVW
    wg = wb[:, o:o + 3 * NH]
    o += 3 * NH
    parts["gab"] = wb[:, o:o + 2 * d]
    parts["g_kv"] = jnp.pad(wg.reshape(-1, KVH, 3 * GRP), ((0, 0), (0, 0), (0, HEAD - 3 * GRP))).reshape(-1, KVH * HEAD)
    parts["g_head"] = jnp.pad(wg.reshape(-1, NH, 3), ((0, 0), (0, 0), (0, HEAD - 3))).reshape(-1, NH * HEAD)
    return parts


def _mixer_proj(h, wp, qk_gain_l, q_dtype):
    g = lambda i: qk_gain_l[i].reshape(1, HEAD)
    return dict(
        zh=mm(h, wp["zh"]),
        q=mm(h, wp["q"], out_dtype=q_dtype, head_gain=g(0)),
        kc=mm(h, wp["kc"]), vc=mm(h, wp["vc"]),
        ks=mm(h, wp["ks"], head_gain=g(2)), vs=mm(h, wp["vs"]),
        kw=mm(h, wp["kw"], head_gain=g(3)), vw=mm(h, wp["vw"]),
        gab=mm(h, wp["gab"]),
    )


def kernel(x_prompt, x_sample, cache_cmp_k, cache_cmp_v, cache_sel_k, cache_sel_v, cache_win_k, cache_win_v,
           state_hgrn, page_table, c_prompt, c_sample, w_ada, b_ada, norm_gain, w_in, lb_param, hgrn_gain,
           qk_gain, w_phi, rel_bias, w_proj_a, w_proj_b, w_out, w_ffn_in, w_ffn_out):
    nb, t_len, d = x_prompt.shape
    n = x_sample.shape[0]
    depth = w_in.shape[0]
    n_pages = page_table.shape[1]
    past = n_pages * PAGE
    w_len = cache_win_k.shape[2]
    n_slc_p = t_len // SEL_BLOCK
    n_past_blk = past // SEL_BLOCK
    n_slc_s = n_past_blk + 1
    s_pad = -(-n_slc_s // SEL_BLOCK) * SEL_BLOCK
    wb = min(WINDOW, t_len)

    sm = jax.nn.softmax(lb_param.astype(F32), axis=0)
    lower = jnp.cumsum(sm, axis=0) - sm[0]
    table_t = rel_bias.astype(F32).T
    near, sat, tiles = _prompt_bias(table_t)
    far = jnp.broadcast_to(sat[:, None, None], (NH, 1, t_len // CMP_STRIDE))
    bias_cs, bias_ss, bias_ws, bias0 = _sample_bias(table_t, past, w_len)
    qpos_p = jnp.arange(t_len, dtype=jnp.int32).reshape(1, t_len)
    qpos_s = jnp.full((1, n * KVH), past, jnp.int32)

    c_rows = nb + n
    c_pad = -(-c_rows // 8) * 8
    c_all = jnp.concatenate([c_prompt, c_sample, jnp.zeros((c_pad - c_rows, d), F32)], axis=0)

    xp = x_prompt.reshape(nb * t_len, d)
    xs = x_sample.reshape(n, d)
    outs_p = [[] for _ in range(7)]
    outs_s = [[] for _ in range(7)]

    for l in range(depth):
        wp = _split_w_in(w_in[l], d)
        w_pa, w_pb, w_o = w_proj_a[l].astype(BF16), w_proj_b[l].astype(BF16), w_out[l].astype(BF16)
        w1, w2 = w_ffn_in[l].astype(BF16), w_ffn_out[l].astype(BF16)
        wphi = w_phi[l].astype(BF16)
        ada = ada_all(c_all, w_ada[l].astype(BF16), b_ada[l].reshape(1, -1))
        hg_gain = hgrn_gain[l].reshape(1, HEAD)
        low = lower[l].reshape(1, d)
        gk_cmp = qk_gain[l, 1].reshape(1, HEAD)

        sh1, sc1, g1, sh2, sc2, g2 = (a.reshape(nb, 1, d) for a in jnp.split(ada[:nb], 6, axis=-1))
        h = modulate(xp, norm_gain[l, 0].reshape(1, d), sc1, sh1, t_len)
        z = _mixer_proj(h, wp, qk_gain[l], BF16)
        gates = mm(h, wp["g_kv"])
        o_a, s_new = hgrn_prompt(z["zh"], low, hg_gain, nb, t_len)
        half = lambda a: a.reshape(nb, t_len // CMP_STRIDE, CMP_STRIDE * KVW)
        kcc = compress(half(z["kc"]), wphi[0], gk_cmp, norm=True)
        vcc = compress(half(z["vc"]), wphi[1], gk_cmp)
        o_cmp, imp = nsa_cmp_prompt(z["q"], kcc, vcc, near, far, nb, t_len)
        sel = topk_mask(imp.reshape(nb * KVH, 1, n_slc_p, t_len), qpos_p, n_slc_p)
        sel = jnp.swapaxes(sel.reshape(nb, KVH, n_slc_p, t_len), 2, 3)
        seq = lambda a: a.reshape(nb, t_len, KVW)
        o_b = nsa_selwin_prompt(z["q"], seq(z["ks"]), seq(z["vs"]), seq(z["kw"]), seq(z["vw"]), sel, tiles,
                                o_cmp, gates, nb, t_len)
        mrg = merge_gate(o_a, o_b, w_pa, w_pb, z["gab"])
        xp = mm_residual(mrg, w_o, xp, g1, t_len)
        h = modulate(xp, norm_gain[l, 1].reshape(1, d), sc2, sh2, t_len)
        xp = mm_residual(ffn_in(h, w1), w2, xp, g2, t_len)
        rows = lambda a: a.reshape(nb, t_len, KVH, HEAD)
        for lst, a in zip(outs_p, (rows(z["kc"]), rows(z["vc"]), rows(z["ks"]), rows(z["vs"]),
                                   rows(z["kw"])[:, -wb:], rows(z["vw"])[:, -wb:], s_new)):
            lst.append(a)

        sh1, sc1, g1, sh2, sc2, g2 = jnp.split(ada[nb:nb + n], 6, axis=-1)
        h = modulate(xs, norm_gain[l, 0].reshape(1, d), sc1, sh1, 1)
        z = _mixer_proj(h, wp, qk_gain[l], F32)
        gates = mm(h, wp["g_head"]).reshape(n, NH, HEAD)
        o_a, s_new = hgrn_sample(z["zh"], low, hg_gain, state_hgrn[l])
        halves = lambda pool: gather_pages(pool, page_table, l).reshape(n, past // CMP_STRIDE, CMP_STRIDE * KVW)
        new = lambda a: a.reshape(n, 1, KVW)
        kcc = compress(halves(cache_cmp_k), wphi[0], gk_cmp, tail=new(tail_proj(z["kc"], wphi[0])), norm=True, sb=2)
        vcc = compress(halves(cache_cmp_v), wphi[1], gk_cmp, tail=new(tail_proj(z["vc"], wphi[1])), sb=2)
        q3 = z["q"].reshape(n, NH, HEAD)
        o_cmp, imp = nsa_cmp_sample(q3, kcc, vcc, bias_cs, s_pad)
        imp = imp.reshape(n, KVH, GRP, s_pad).transpose(2, 3, 0, 1).reshape(1, GRP, s_pad, n * KVH)
        sel = topk_mask(imp, qpos_s, n_slc_s)
        sel = jnp.repeat(sel.reshape(s_pad, n, KVH).transpose(1, 2, 0), GRP, axis=1)
        past_rows = lambda pool: gather_pages(pool, page_table, l).reshape(n, past, KVW)
        win = lambda c: c[l].reshape(n, w_len, KVW)
        o_b, wk_new, wv_new = nsa_selwin_sample(
            q3, past_rows(cache_sel_k), past_rows(cache_sel_v), new(z["ks"]), new(z["vs"]), sel, bias_ss,
            win(cache_win_k), win(cache_win_v), new(z["kw"]), new(z["vw"]), bias_ws, bias0, o_cmp, gates, n_past_blk)
        mrg = merge_gate(o_a, o_b.reshape(n, d), w_pa, w_pb, z["gab"])
        xs = mm_residual(mrg, w_o, xs, g1, 1)
        h = modulate(xs, norm_gain[l, 1].reshape(1, d), sc2, sh2, 1)
        xs = mm_residual(ffn_in(h, w1), w2, xs, g2, 1)
        rows = lambda a: a.reshape(n, 1, KVH, HEAD)
        wrow = lambda a: a.reshape(n, w_len, KVH, HEAD)
        for lst, a in zip(outs_s, (rows(z["kc"]), rows(z["vc"]), rows(z["ks"]), rows(z["vs"]),
                                   wrow(wk_new), wrow(wv_new), s_new)):
            lst.append(a)

    return ((xp.reshape(nb, t_len, d), xs.reshape(n, 1, d))
            + tuple(jnp.stack(a) for a in outs_p) + tuple(jnp.stack(a) for a in outs_s))
```

```python
import functools
import math

import numpy as np
import jax
import jax.numpy as jnp
from jax import lax
from jax.experimental import pallas as pl
from jax.experimental.pallas import tpu as pltpu

F32 = jnp.float32
BF16 = jnp.bfloat16

HEAD = 128
KVH = 4
GRP = 4
NH = KVH * GRP
KVW = KVH * HEAD
HG_CHUNK = 64
HG_SUB = 16
CMP_STRIDE = 16
CMP_BLOCK = 32
SEL_BLOCK = 64
N_SELECT = 16
WINDOW = 512
QB = 128
PAGE = 128
N_BUCKETS = 32
MAX_DISTANCE = 128
EPS = 1e-6
NEG = -1e30
SELECT_BONUS = 1e3
SCALE = HEAD ** -0.5
VMEM_LIMIT = 56 * 1024 * 1024


def _cparams(sem):
    return pltpu.CompilerParams(dimension_semantics=sem, vmem_limit_bytes=VMEM_LIMIT)


def _bucket_np(dist):
    n = np.maximum(np.asarray(dist, np.int64), 0)
    max_exact = N_BUCKETS // 2
    nf = np.maximum(n, 1).astype(np.float32)
    large = max_exact + (np.log(nf / np.float32(max_exact)) / np.float32(math.log(MAX_DISTANCE / max_exact))
                         * np.float32(N_BUCKETS - max_exact)).astype(np.int32)
    return np.where(n < max_exact, n, np.minimum(large, N_BUCKETS - 1)).astype(np.int32)


def _split3(x):
    hi = x.astype(BF16)
    r1 = x - hi.astype(F32)
    mid = r1.astype(BF16)
    lo = (r1 - mid.astype(F32)).astype(BF16)
    return hi, mid, lo


def _dot(a, b):
    return jnp.dot(a, b, preferred_element_type=F32)


def _dot_nt(a, b):
    return lax.dot_general(a, b, (((1,), (1,)), ((), ())), preferred_element_type=F32)


def _dot_tn(a, b):
    return lax.dot_general(a, b, (((0,), (0,)), ((), ())), preferred_element_type=F32)


def _dot_exact_rhs(a01, x):
    hi, mid, lo = _split3(x)
    a = a01.astype(BF16)
    return _dot(a, hi) + _dot(a, mid) + _dot(a, lo)


def _dot_nt_exact_rhs(a01, x):
    hi, mid, lo = _split3(x)
    a = a01.astype(BF16)
    return _dot_nt(a, hi) + _dot_nt(a, mid) + _dot_nt(a, lo)


def _sigmoid(x):
    return 1.0 / (1.0 + jnp.exp(-x))


def _silu(x):
    return x * _sigmoid(x)


def _ada_kernel(c_ref, w_ref, b_ref, o_ref):
    c = c_ref[...]
    o_ref[...] = _dot(_silu(c).astype(BF16), w_ref[...]) + b_ref[...]


def ada_all(c, w_bf, b):
    m, d = c.shape
    n = w_bf.shape[1]
    tn = 512
    return pl.pallas_call(
        _ada_kernel,
        name="ada",
        grid=(n // tn,),
        in_specs=[pl.BlockSpec((m, d), lambda j: (0, 0)),
                  pl.BlockSpec((d, tn), lambda j: (0, j)),
                  pl.BlockSpec((1, tn), lambda j: (0, j))],
        out_specs=pl.BlockSpec((m, tn), lambda j: (0, j)),
        out_shape=jax.ShapeDtypeStruct((m, n), F32),
        compiler_params=_cparams(("parallel",)),
    )(c, w_bf, b)


def _group_spec(rows_per_group, tm, tn, col_block=None):
    def imap(i, j):
        return ((i * tm) // rows_per_group, 0, j if col_block is None else col_block)
    return pl.BlockSpec((None, 1, tn), imap)


def _modulate_kernel(x_ref, gain_ref, scale_ref, shift_ref, o_ref):
    x = x_ref[...]
    y = x * lax.rsqrt(jnp.mean(x * x, axis=-1, keepdims=True) + EPS)
    y = y * gain_ref[...]
    o_ref[...] = (y * (1.0 + scale_ref[...]) + shift_ref[...]).astype(o_ref.dtype)


def modulate(x, gain, scale, shift, rows_per_group):
    m, d = x.shape
    tm = min(256, rows_per_group) if rows_per_group > 1 else min(256, m)
    if rows_per_group == 1:
        vec = pl.BlockSpec((tm, d), lambda i: (i, 0))
        scale, shift = scale.reshape(m, d), shift.reshape(m, d)
    else:
        vec = pl.BlockSpec((None, 1, d), lambda i: ((i * tm) // rows_per_group, 0, 0))
    return pl.pallas_call(
        _modulate_kernel,
        name="modulate",
        grid=(m // tm,),
        in_specs=[pl.BlockSpec((tm, d), lambda i: (i, 0)),
                  pl.BlockSpec((1, d), lambda i: (0, 0)), vec, vec],
        out_specs=pl.BlockSpec((tm, d), lambda i: (i, 0)),
        out_shape=jax.ShapeDtypeStruct((m, d), BF16),
        compiler_params=_cparams(("parallel",)),
    )(x, gain, scale, shift)


def _mm_kernel(a_ref, b_ref, o_ref):
    o_ref[...] = _dot(a_ref[...], b_ref[...]).astype(o_ref.dtype)


def _mm_headnorm_kernel(a_ref, b_ref, g_ref, o_ref):
    z = _dot(a_ref[...], b_ref[...])
    tm, tn = z.shape
    for h in range(tn // HEAD):
        zh = z[:, h * HEAD:(h + 1) * HEAD]
        y = zh * lax.rsqrt(jnp.mean(zh * zh, axis=-1, keepdims=True) + EPS)
        o_ref[:, h * HEAD:(h + 1) * HEAD] = (y * g_ref[...]).astype(o_ref.dtype)


def _pick_tm(m):
    for tm in (1024, 512, 256, 128):
        if m % tm == 0:
            return tm
    return m


def mm(a, b, out_dtype=F32, head_gain=None, tn=512):
    m, k = a.shape
    n = b.shape[1]
    tn = min(tn, n)
    tm = _pick_tm(m)
    in_specs = [pl.BlockSpec((tm, k), lambda i, j: (i, 0)),
                pl.BlockSpec((k, tn), lambda i, j: (0, j))]
    args = [a, b]
    kern = _mm_kernel
    if head_gain is not None:
        in_specs.append(pl.BlockSpec((1, HEAD), lambda i, j: (0, 0)))
        args.append(head_gain)
        kern = _mm_headnorm_kernel
    return pl.pallas_call(
        kern,
        name="mm_headnorm" if head_gain is not None else "mm",
        grid=(m // tm, n // tn),
        in_specs=in_specs,
        out_specs=pl.BlockSpec((tm, tn), lambda i, j: (i, j)),
        out_shape=jax.ShapeDtypeStruct((m, n), out_dtype),
        compiler_params=_cparams(("parallel", "parallel")),
    )(*args)


def _merge_kernel(oa_ref, ob_ref, wa_ref, wb_ref, ga_ref, gb_ref, o_ref):
    ya = _dot(oa_ref[...], wa_ref[...])
    yb = _dot(ob_ref[...], wb_ref[...])
    o_ref[...] = (_sigmoid(ga_ref[...]) * ya + _sigmoid(gb_ref[...]) * yb).astype(o_ref.dtype)


def merge_gate(o_a, o_b, w_pa, w_pb, gab):
    m, k = o_a.shape
    n = w_pa.shape[1]
    tn = 512
    tm = _pick_tm(m)
    nb = n // tn
    return pl.pallas_call(
        _merge_kernel,
        name="merge_gate",
        grid=(m // tm, nb),
        in_specs=[pl.BlockSpec((tm, k), lambda i, j: (i, 0)),
                  pl.BlockSpec((tm, k), lambda i, j: (i, 0)),
                  pl.BlockSpec((k, tn), lambda i, j: (0, j)),
                  pl.BlockSpec((k, tn), lambda i, j: (0, j)),
                  pl.BlockSpec((tm, tn), lambda i, j: (i, j)),
                  pl.BlockSpec((tm, tn), lambda i, j: (i, j + nb))],
        out_specs=pl.BlockSpec((tm, tn), lambda i, j: (i, j)),
        out_shape=jax.ShapeDtypeStruct((m, n), BF16),
        compiler_params=_cparams(("parallel", "parallel")),
    )(o_a, o_b, w_pa, w_pb, gab, gab)


def _mm_res_kernel(a_ref, b_ref, x_ref, g_ref, o_ref):
    o_ref[...] = x_ref[...] + g_ref[...] * _dot(a_ref[...], b_ref[...])


def mm_residual(a, b, x, g, rows_per_group):
    m, k = a.shape
    n = b.shape[1]
    tn = 512
    tm = min(_pick_tm(m), rows_per_group) if rows_per_group > 1 else _pick_tm(m)
    if rows_per_group == 1:
        gspec = pl.BlockSpec((tm, tn), lambda i, j: (i, j))
        g = g.reshape(m, n)
    else:
        gspec = _group_spec(rows_per_group, tm, tn)
    return pl.pallas_call(
        _mm_res_kernel,
        name="mm_residual",
        grid=(m // tm, n // tn),
        in_specs=[pl.BlockSpec((tm, k), lambda i, j: (i, 0)),
                  pl.BlockSpec((k, tn), lambda i, j: (0, j)),
                  pl.BlockSpec((tm, tn), lambda i, j: (i, j)),
                  gspec],
        out_specs=pl.BlockSpec((tm, tn), lambda i, j: (i, j)),
        out_shape=jax.ShapeDtypeStruct((m, n), F32),
        compiler_params=_cparams(("parallel", "parallel")),
    )(a, b, x, g)


def _ffn_in_kernel(a_ref, bg_ref, bu_ref, o_ref):
    a = a_ref[...]
    gate = _dot(a, bg_ref[...])
    up = _dot(a, bu_ref[...])
    o_ref[...] = (_silu(gate) * up).astype(o_ref.dtype)


def ffn_in(a, w1):
    m, k = a.shape
    f = w1.shape[1] // 2
    tn = 512
    nb = f // tn
    tm = _pick_tm(m)
    return pl.pallas_call(
        _ffn_in_kernel,
        name="ffn_in",
        grid=(m // tm, nb),
        in_specs=[pl.BlockSpec((tm, k), lambda i, j: (i, 0)),
                  pl.BlockSpec((k, tn), lambda i, j: (0, j)),
                  pl.BlockSpec((k, tn), lambda i, j: (0, j + nb))],
        out_specs=pl.BlockSpec((tm, tn), lambda i, j: (i, j)),
        out_shape=jax.ShapeDtypeStruct((m, f), BF16),
        compiler_params=_cparams(("parallel", "parallel")),
    )(a, w1, w1)


def _hgrn_out(o, gain, zg):
    y = o * lax.rsqrt(jnp.mean(o * o, axis=-1, keepdims=True) + EPS)
    return y * gain * _silu(zg)


def _hgrn_prompt_kernel(zq_ref, zf_ref, zi_ref, zg_ref, lower_ref, gain_ref, o_ref, s_ref, state, *, tt):
    t = pl.program_id(2)

    @pl.when(t == 0)
    def _():
        state[...] = jnp.zeros_like(state)

    lower = lower_ref[...]
    gain = gain_ref[...]
    c = HG_CHUNK
    tril = (lax.broadcasted_iota(jnp.int32, (c, c), 0) >= lax.broadcasted_iota(jnp.int32, (c, c), 1)).astype(F32)
    tio = lax.broadcasted_iota(jnp.int32, (HG_SUB, 1), 0)

    def chunk(ci, carry):
        r0 = pl.multiple_of(ci * c, c)
        q = zq_ref[pl.ds(r0, c), :]
        f = lower + (1.0 - lower) * _sigmoid(zf_ref[pl.ds(r0, c), :])
        k = 1.0 - f
        v = zi_ref[pl.ds(r0, c), :]
        b = _dot_exact_rhs(tril, jnp.log(f))
        s_old = state[...]
        o_inter = _dot((q * jnp.exp(b)).astype(BF16), s_old.astype(BF16))
        blast = b[c - 1:c, :]
        kd = k * jnp.exp(blast - b)
        decay_col = jnp.transpose(jnp.broadcast_to(jnp.exp(blast), (HEAD, HEAD)))
        state[...] = decay_col * s_old + _dot_tn(kd.astype(BF16), v.astype(BF16))
        v_bf = v.astype(BF16)
        for i in range(c // HG_SUB):
            lo = i * HG_SUB
            qi, ki, vi, bi = (a[lo:lo + HG_SUB, :] for a in (q, k, v, b))
            acc = o_inter[lo:lo + HG_SUB, :]
            if i > 0:
                r = b[lo - 1:lo, :]
                qd = qi * jnp.exp(bi - r)
                kp = k[:lo, :] * jnp.exp(r - b[:lo, :])
                a_mat = _dot_nt(qd.astype(BF16), kp.astype(BF16))
                acc = acc + _dot(a_mat.astype(BF16), v_bf[:lo, :])
            for s in range(HG_SUB):
                e = jnp.exp(jnp.where(tio >= s, bi - bi[s:s + 1, :], NEG))
                w = jnp.sum(qi * ki[s:s + 1, :] * e, axis=-1, keepdims=True)
                acc = acc + w * vi[s:s + 1, :]
            zg = zg_ref[pl.ds(pl.multiple_of(r0 + lo, HG_SUB), HG_SUB), :]
            o_ref[pl.ds(pl.multiple_of(r0 + lo, HG_SUB), HG_SUB), :] = _hgrn_out(acc, gain, zg).astype(o_ref.dtype)
        return carry

    lax.fori_loop(0, tt // c, chunk, 0)

    @pl.when(t == pl.num_programs(2) - 1)
    def _():
        s_ref[...] = state[...]


def hgrn_prompt(zh, lower, gain, nb, t_len):
    m = zh.shape[0]
    nh = zh.shape[1] // (4 * HEAD)
    tt = min(512, t_len)
    ntt = t_len // tt

    def spec(part):
        return pl.BlockSpec((tt, HEAD), lambda b, h, t: (b * ntt + t, part * nh + h))

    return pl.pallas_call(
        functools.partial(_hgrn_prompt_kernel, tt=tt),
        name="hgrn_prompt",
        grid=(nb, nh, ntt),
        in_specs=[spec(0), spec(1), spec(2), spec(3),
                  pl.BlockSpec((1, HEAD), lambda b, h, t: (0, h)),
                  pl.BlockSpec((1, HEAD), lambda b, h, t: (0, 0))],
        out_specs=[pl.BlockSpec((tt, HEAD), lambda b, h, t: (b * ntt + t, h)),
                   pl.BlockSpec((None, None, HEAD, HEAD), lambda b, h, t: (b, h, 0, 0))],
        out_shape=[jax.ShapeDtypeStruct((m, nh * HEAD), BF16),
                   jax.ShapeDtypeStruct((nb, nh, HEAD, HEAD), F32)],
        scratch_shapes=[pltpu.VMEM((HEAD, HEAD), F32)],
        compiler_params=_cparams(("parallel", "parallel", "arbitrary")),
    )(zh, zh, zh, zh, lower, gain)


def _hgrn_sample_kernel(zqt_ref, zft_ref, zi_ref, zg_ref, lowt_ref, gain_ref, s_in_ref, o_ref, s_out_ref, o_acc):
    n = s_in_ref.shape[0]
    f_t = lowt_ref[...] + (1.0 - lowt_ref[...]) * _sigmoid(zft_ref[...])
    qf_t = zqt_ref[...]
    k_t = 1.0 - f_t
    for s in range(n):
        s_new = f_t[:, s:s + 1] * s_in_ref[s] + k_t[:, s:s + 1] * zi_ref[s:s + 1, :]
        s_out_ref[s] = s_new
        o_acc[s:s + 1, :] = jnp.sum(qf_t[:, s:s + 1] * s_new, axis=0, keepdims=True)
    o_ref[...] = _hgrn_out(o_acc[...], gain_ref[...], zg_ref[...]).astype(o_ref.dtype)


def hgrn_sample(zh, lower, gain, state, layer):
    n = zh.shape[0]
    nh = state.shape[2]
    d = nh * HEAD
    zq_t = zh[:, :d].reshape(n, nh, HEAD).transpose(1, 2, 0)
    zf_t = zh[:, d:2 * d].reshape(n, nh, HEAD).transpose(1, 2, 0)
    low_t = lower.reshape(nh, HEAD, 1)
    tspec = pl.BlockSpec((None, HEAD, n), lambda h: (h, 0, 0))
    return pl.pallas_call(
        _hgrn_sample_kernel,
        name="hgrn_sample",
        grid=(nh,),
        in_specs=[tspec, tspec,
                  pl.BlockSpec((n, HEAD), lambda h: (0, 2 * nh + h)),
                  pl.BlockSpec((n, HEAD), lambda h: (0, 3 * nh + h)),
                  pl.BlockSpec((None, HEAD, 1), lambda h: (h, 0, 0)),
                  pl.BlockSpec((1, HEAD), lambda h: (0, 0)),
                  pl.BlockSpec((None, n, None, HEAD, HEAD), lambda h: (layer, 0, h, 0, 0))],
        out_specs=[pl.BlockSpec((n, HEAD), lambda h: (0, h)),
                   pl.BlockSpec((n, None, HEAD, HEAD), lambda h: (0, h, 0, 0))],
        out_shape=[jax.ShapeDtypeStruct((n, d), BF16),
                   jax.ShapeDtypeStruct(state.shape[1:], F32)],
        scratch_shapes=[pltpu.VMEM((n, HEAD), F32)],
        compiler_params=_cparams(("parallel",)),
    )(zq_t, zf_t, zh, zh, low_t, gain, state)


PAGE_ROWS = PAGE * KVH
HALF_ROWS = CMP_STRIDE * KVH
HALVES_PER_PAGE = PAGE // CMP_STRIDE


def _rows_view(pool):
    return pool.reshape(pool.shape[:-3] + (pool.shape[-3] * KVH, HEAD))


def _page_specs(n_pages, layer):
    def spec(pg):
        return pl.BlockSpec((None, None, PAGE_ROWS, HEAD), lambda n, pt: (layer, pt[n, pg], 0, 0))
    return [spec(pg) for pg in range(n_pages)]


def _tail_kernel(x_ref, w_ref, o_ref):
    for k in range(KVH):
        o_ref[:, k * HEAD:(k + 1) * HEAD] = _dot(x_ref[:, k * HEAD:(k + 1) * HEAD].astype(BF16),
                                                 w_ref[k, 0, :, HEAD:])


def tail_proj(new_rows, wcat):
    n = new_rows.shape[0]
    return pl.pallas_call(
        _tail_kernel,
        name="compress_tail",
        out_shape=jax.ShapeDtypeStruct((n, KVW), F32),
        compiler_params=pltpu.CompilerParams(vmem_limit_bytes=VMEM_LIMIT),
    )(new_rows, wcat)


def _head_norm(x, gain):
    return x * lax.rsqrt(jnp.mean(x * x, axis=-1, keepdims=True) + EPS) * gain


def _compress_kernel(x_ref, w_ref, gain_ref, o_ref, *, norm):
    rows = x_ref.shape[0]
    last = lax.broadcasted_iota(jnp.int32, (rows, 1), 0) == rows - 1
    for k in range(KVH):
        u = jnp.zeros((rows, 2 * HEAD), F32)
        for p in range(CMP_STRIDE):
            c0 = p * KVW + k * HEAD
            u = u + _dot(x_ref[:, c0:c0 + HEAD].astype(BF16), w_ref[k, p])
        nxt = pltpu.roll(u[:, HEAD:], rows - 1, axis=0)
        out = jnp.where(last, 0.0, u[:, :HEAD] + nxt)
        if norm:
            out = _head_norm(out, gain_ref[...])
        o_ref[:, k * HEAD:(k + 1) * HEAD] = out


def compress(x, wcat, gain, norm=False):
    s, rows, width = x.shape
    return pl.pallas_call(
        functools.partial(_compress_kernel, norm=norm),
        name="compress_prompt",
        grid=(s,),
        in_specs=[pl.BlockSpec((None, rows, width), lambda i: (i, 0, 0)),
                  pl.BlockSpec(wcat.shape, lambda i: (0, 0, 0, 0)),
                  pl.BlockSpec((1, HEAD), lambda i: (0, 0))],
        out_specs=pl.BlockSpec((None, rows, KVW), lambda i: (i, 0, 0)),
        out_shape=jax.ShapeDtypeStruct((s, rows, KVW), F32),
        compiler_params=_cparams(("parallel",)),
    )(x, wcat, gain)


def _compress_sample_kernel(pt_ref, *refs, n_pages, norm):
    pages = refs[:n_pages]
    w_ref, gain_ref, tail_ref, o_ref = refs[n_pages:]
    rows = n_pages * HALVES_PER_PAGE
    last = lax.broadcasted_iota(jnp.int32, (rows, 1), 0) == rows - 1
    for k in range(KVH):
        u = jnp.zeros((rows, 2 * HEAD), F32)
        for p in range(CMP_STRIDE):
            xs = jnp.concatenate([pg[pl.ds(p * KVH + k, HALVES_PER_PAGE, stride=HALF_ROWS), :] for pg in pages], axis=0)
            u = u + _dot(xs.astype(BF16), w_ref[k, p])
        nxt = pltpu.roll(u[:, HEAD:], rows - 1, axis=0)
        out = u[:, :HEAD] + jnp.where(last, tail_ref[:, k * HEAD:(k + 1) * HEAD], nxt)
        if norm:
            out = _head_norm(out, gain_ref[...])
        o_ref[:, k * HEAD:(k + 1) * HEAD] = out


def compress_sample(pool_rows, page_table, layer, wcat, gain, tail, norm=False):
    n, n_pages = page_table.shape
    rows = n_pages * HALVES_PER_PAGE
    return pl.pallas_call(
        functools.partial(_compress_sample_kernel, n_pages=n_pages, norm=norm),
        name="compress_sample",
        grid_spec=pltpu.PrefetchScalarGridSpec(
            num_scalar_prefetch=1,
            grid=(n,),
            in_specs=_page_specs(n_pages, layer) + [
                pl.BlockSpec(wcat.shape, lambda i, pt: (0, 0, 0, 0)),
                pl.BlockSpec((1, HEAD), lambda i, pt: (0, 0)),
                pl.BlockSpec((None, 1, KVW), lambda i, pt: (i, 0, 0))],
            out_specs=pl.BlockSpec((None, rows, KVW), lambda i, pt: (i, 0, 0))),
        out_shape=jax.ShapeDtypeStruct((n, rows, KVW), F32),
        compiler_params=_cparams(("parallel",)),
    )(page_table, *([pool_rows] * n_pages), wcat, gain, tail)


def _overlap(n_slc, ncp):
    c_start = lax.broadcasted_iota(jnp.int32, (n_slc, ncp), 1) * CMP_STRIDE
    s_start = lax.broadcasted_iota(jnp.int32, (n_slc, ncp), 0) * SEL_BLOCK
    return jnp.where(c_start < s_start + SEL_BLOCK, jnp.where(c_start + CMP_BLOCK > s_start, 1.0, 0.0), 0.0)


def _masked_softmax(s, valid):
    s = jnp.where(valid, s, NEG)
    m = jnp.max(s, axis=-1, keepdims=True)
    p = jnp.where(valid, jnp.exp(s - m), 0.0)
    return p / jnp.maximum(jnp.sum(p, axis=-1, keepdims=True), 1e-30)


NEAR_W = 32
NEAR_BACK = 16


def _dot_exact_lhs(x, b01):
    hi, mid, lo = _split3(x)
    return _dot(hi, b01) + _dot(mid, b01) + _dot(lo, b01)


def _nsa_cmp_prompt_kernel(q_ref, kc_ref, vc_ref, near_ref, far_ref, o_ref, imp_ref, *, n_slc):
    i = pl.program_id(2)
    ncp = kc_ref.shape[0]
    kc = kc_ref[...].astype(BF16)
    vc = vc_ref[...].astype(BF16)
    qpos = i * QB + lax.broadcasted_iota(jnp.int32, (QB, ncp), 0)
    cend = lax.broadcasted_iota(jnp.int32, (QB, ncp), 1) * CMP_STRIDE + (CMP_BLOCK - 1)
    valid = qpos >= cend
    rel = lax.broadcasted_iota(jnp.int32, (NEAR_W, ncp), 1) - i * (QB // CMP_STRIDE) + NEAR_BACK
    place = jnp.where(rel == lax.broadcasted_iota(jnp.int32, (NEAR_W, ncp), 0), 1.0, 0.0).astype(BF16)
    psum = jnp.zeros((QB, ncp), F32)
    for g in range(GRP):
        bias = far_ref[g] + _dot_exact_lhs(near_ref[g], place)
        s = _dot_nt(q_ref[:, g * HEAD:(g + 1) * HEAD], kc) * SCALE + bias
        p = _masked_softmax(s, valid)
        o_ref[:, g * HEAD:(g + 1) * HEAD] = _dot(p.astype(BF16), vc)
        psum = psum + p
    imp_ref[...] = _dot_nt_exact_rhs(_overlap(n_slc, ncp), psum)


def nsa_cmp_prompt(q, kcc, vcc, near, far, nb, t_len):
    n_qb = t_len // QB
    ncp = kcc.shape[1]
    n_slc = t_len // SEL_BLOCK
    return pl.pallas_call(
        functools.partial(_nsa_cmp_prompt_kernel, n_slc=n_slc),
        name="nsa_cmp_prompt",
        grid=(nb, KVH, n_qb),
        in_specs=[pl.BlockSpec((QB, GRP * HEAD), lambda b, k, i: (b * n_qb + i, k)),
                  pl.BlockSpec((None, ncp, HEAD), lambda b, k, i: (b, 0, k)),
                  pl.BlockSpec((None, ncp, HEAD), lambda b, k, i: (b, 0, k)),
                  pl.BlockSpec((GRP, QB, NEAR_W), lambda b, k, i: (k, 0, 0)),
                  pl.BlockSpec((GRP, 1, ncp), lambda b, k, i: (k, 0, 0))],
        out_specs=[pl.BlockSpec((QB, GRP * HEAD), lambda b, k, i: (b * n_qb + i, k)),
                   pl.BlockSpec((None, None, n_slc, QB), lambda b, k, i: (b, k, 0, i))],
        out_shape=[jax.ShapeDtypeStruct((nb * t_len, NH * HEAD), F32),
                   jax.ShapeDtypeStruct((nb, KVH, n_slc, t_len), F32)],
        compiler_params=_cparams(("parallel", "parallel", "parallel")),
    )(q, kcc, vcc, near, far)


def _topk_kernel(imp_ref, qpos_ref, o_ref, *, n_slc):
    imp = imp_ref[0]
    for g in range(1, imp_ref.shape[0]):
        imp = imp + imp_ref[g]
    blk = lax.broadcasted_iota(jnp.int32, imp.shape, 0)
    qpos = qpos_ref[...]
    cur = lax.shift_right_logical(qpos, 6)
    forced = jnp.where(blk == 0, 1.0, jnp.where(blk == cur, 1.0, jnp.where(blk == cur - 1, 1.0, 0.0)))
    score = jnp.where(blk * SEL_BLOCK <= qpos, imp + SELECT_BONUS * forced, NEG)
    rank = jnp.zeros(imp.shape, F32)
    for s2 in range(n_slc):
        other = score[s2:s2 + 1, :]
        rank = rank + jnp.where(other > score, 1.0, jnp.where(other == score, jnp.where(blk > s2, 1.0, 0.0), 0.0))
    keep = jnp.where(rank < float(min(N_SELECT, n_slc)), jnp.where(blk < n_slc, 1.0, 0.0), 0.0)
    o_ref[...] = keep


def topk_mask(imp, qpos, n_slc):
    r, g, s, q = imp.shape
    tq = _pick_tm(q) if q % QB == 0 else q
    tq = min(tq, 512)
    return pl.pallas_call(
        functools.partial(_topk_kernel, n_slc=n_slc),
        name="topk_mask",
        grid=(r, q // tq),
        in_specs=[pl.BlockSpec((None, g, s, tq), lambda a, t: (a, 0, 0, t)),
                  pl.BlockSpec((1, tq), lambda a, t: (0, t))],
        out_specs=pl.BlockSpec((None, s, tq), lambda a, t: (a, 0, t)),
        out_shape=jax.ShapeDtypeStruct((r, s, q), F32),
        compiler_params=_cparams(("parallel", "parallel")),
    )(imp, qpos)


N_BIAS_TILES = 6


def _nsa_selwin_prompt_kernel(q_ref, ks_ref, vs_ref, kw_ref, vw_ref, mask_ref, expand_ref, bias_ref, ocmp_ref,
                              gate_ref, o_ref, s_buf, m_buf, l_buf, acc_buf):
    i = pl.program_id(2)
    rows = GRP * QB
    pair = 2 * QB
    q = jnp.concatenate([q_ref[:, g * HEAD:(g + 1) * HEAD] for g in range(GRP)], axis=0)
    mask_bf = mask_ref[...].astype(BF16)
    masked_tile = N_BIAS_TILES - 1

    def attend(k_ref, v_ref, kt_lo, selected):
        m_buf[...] = jnp.full(m_buf.shape, -jnp.inf, F32)
        l_buf[...] = jnp.zeros(l_buf.shape, F32)
        acc_buf[...] = jnp.zeros(acc_buf.shape, F32)
        pair_lo = kt_lo // 2
        pair_hi = i // 2 + 1

        def tile_bias(kt):
            delta = i - kt
            if selected:
                idx = jnp.where(delta < 0, masked_tile, jnp.minimum(delta, 2))
            else:
                idx = jnp.where(delta < 0, masked_tile, jnp.where(delta > WINDOW // QB, masked_tile, delta))
            return bias_ref[idx]

        def scores(pt, mx):
            k0 = pl.multiple_of(pt * pair, pair)
            kk = k_ref[pl.ds(k0, pair), :].astype(BF16)
            s = _dot_nt(q, kk) * SCALE
            for j in range(2):
                sj = s[:, j * QB:(j + 1) * QB].reshape(GRP, QB, QB) + tile_bias(2 * pt + j)
                if selected:
                    chosen = _dot(mask_bf, expand_ref[:, pl.ds(pl.multiple_of(k0 + j * QB, QB), QB)])
                    sj = sj + ((chosen - 1.0) * 1e30)[None]
                sj = sj.reshape(rows, QB)
                s_buf[2 * pt + j] = sj
                mx = jnp.maximum(mx, sj)
            return mx

        def pass1(pt, carry):
            m_buf[...] = scores(pt, m_buf[...])
            return carry

        lax.fori_loop(pair_lo, pair_hi, pass1, 0)
        m_buf[...] = jnp.broadcast_to(jnp.max(m_buf[...], axis=-1, keepdims=True), m_buf.shape)

        def pass2(pt, carry):
            k0 = pl.multiple_of(pt * pair, pair)
            vv = v_ref[pl.ds(k0, pair), :].astype(BF16)
            m = m_buf[...]
            p0 = jnp.exp(s_buf[2 * pt] - m)
            p1 = jnp.exp(s_buf[2 * pt + 1] - m)
            l_buf[...] = l_buf[...] + (p0 + p1)
            p = jnp.concatenate([p0.astype(BF16), p1.astype(BF16)], axis=1)
            acc_buf[...] = acc_buf[...] + _dot(p, vv)
            return carry

        lax.fori_loop(pair_lo, pair_hi, pass2, 0)
        return acc_buf[...] / jnp.sum(l_buf[...], axis=-1, keepdims=True)

    o_sel = attend(ks_ref, vs_ref, 0, True)
    o_win = attend(kw_ref, vw_ref, jnp.maximum(i - WINDOW // QB, 0), False)
    gate = _sigmoid(gate_ref[...])
    for g in range(GRP):
        rs = slice(g * QB, (g + 1) * QB)
        cs = slice(g * HEAD, (g + 1) * HEAD)
        o = (gate[:, 3 * g:3 * g + 1] * ocmp_ref[:, cs] + gate[:, 3 * g + 1:3 * g + 2] * o_sel[rs, :]
             + gate[:, 3 * g + 2:3 * g + 3] * o_win[rs, :])
        o_ref[:, cs] = o.astype(o_ref.dtype)


def nsa_selwin_prompt(q, ks, vs, kw, vw, mask, tiles, o_cmp, gates, nb, t_len):
    assert t_len % (2 * QB) == 0
    n_qb = t_len // QB
    n_slc = t_len // SEL_BLOCK
    rows = GRP * QB
    key_blk = lax.shift_right_logical(lax.broadcasted_iota(jnp.int32, (n_slc, t_len), 1), 6)
    expand = (key_blk == lax.broadcasted_iota(jnp.int32, (n_slc, t_len), 0)).astype(BF16)
    kv_spec = pl.BlockSpec((None, t_len, HEAD), lambda b, k, i: (b, 0, k))
    q_spec = pl.BlockSpec((QB, GRP * HEAD), lambda b, k, i: (b * n_qb + i, k))
    return pl.pallas_call(
        _nsa_selwin_prompt_kernel,
        name="nsa_selwin_prompt",
        grid=(nb, KVH, n_qb),
        in_specs=[q_spec, kv_spec, kv_spec, kv_spec, kv_spec,
                  pl.BlockSpec((None, None, QB, n_slc), lambda b, k, i: (b, k, i, 0)),
                  pl.BlockSpec((n_slc, t_len), lambda b, k, i: (0, 0)),
                  pl.BlockSpec((N_BIAS_TILES, GRP, QB, QB), lambda b, k, i: (0, k, 0, 0)),
                  q_spec,
                  pl.BlockSpec((QB, HEAD), lambda b, k, i: (b * n_qb + i, k))],
        out_specs=q_spec,
        out_shape=jax.ShapeDtypeStruct((nb * t_len, NH * HEAD), BF16),
        scratch_shapes=[pltpu.VMEM((n_qb, rows, QB), F32), pltpu.VMEM((rows, QB), F32),
                        pltpu.VMEM((rows, QB), F32), pltpu.VMEM((rows, HEAD), F32)],
        compiler_params=_cparams(("parallel", "parallel", "arbitrary")),
    )(q, ks, vs, kw, vw, mask, expand, tiles, o_cmp, gates)


def _expand_q(q16):
    tiled = jnp.concatenate([q16] * KVH, axis=1)
    head_kv = lax.shift_right_logical(lax.broadcasted_iota(jnp.int32, tiled.shape, 0), 2)
    lane_kv = lax.shift_right_logical(lax.broadcasted_iota(jnp.int32, tiled.shape, 1), 7)
    return jnp.where(head_kv == lane_kv, tiled, 0.0)


def _own_kv_block(o16):
    head_kv = lax.shift_right_logical(lax.broadcasted_iota(jnp.int32, (NH, HEAD), 0), 2)
    out = jnp.zeros((NH, HEAD), F32)
    for k in range(KVH):
        out = out + jnp.where(head_kv == k, o16[:, k * HEAD:(k + 1) * HEAD], 0.0)
    return out


def _nsa_cmp_sample_kernel(q_ref, kc_ref, vc_ref, bias_ref, o_ref, imp_ref):
    sb = q_ref.shape[0]
    ncp = kc_ref.shape[1]
    s_pad = imp_ref.shape[2]
    bias = bias_ref[...]
    valid = bias > 0.5 * NEG
    c_start = lax.broadcasted_iota(jnp.int32, (ncp, s_pad), 0) * CMP_STRIDE
    s_start = lax.broadcasted_iota(jnp.int32, (ncp, s_pad), 1) * SEL_BLOCK
    ovl = jnp.where(c_start < s_start + SEL_BLOCK, jnp.where(c_start + CMP_BLOCK > s_start, 1.0, 0.0), 0.0)
    ovl = ovl.astype(BF16)
    for s in range(sb):
        qx = _expand_q(q_ref[s]).astype(BF16)
        sc = _dot_nt(qx, kc_ref[s].astype(BF16)) * SCALE + bias
        p = _masked_softmax(sc, valid)
        o_ref[s] = _own_kv_block(_dot(p.astype(BF16), vc_ref[s].astype(BF16)))
        hi, mid, lo = _split3(p)
        imp_ref[s] = _dot(hi, ovl) + _dot(mid, ovl) + _dot(lo, ovl)


def nsa_cmp_sample(q, kcc, vcc, bias_cs, s_pad, sb=8):
    n, ncp = kcc.shape[:2]
    return pl.pallas_call(
        _nsa_cmp_sample_kernel,
        name="nsa_cmp_sample",
        grid=(n // sb,),
        in_specs=[pl.BlockSpec((sb, NH, HEAD), lambda i: (i, 0, 0)),
                  pl.BlockSpec((sb, ncp, KVW), lambda i: (i, 0, 0)),
                  pl.BlockSpec((sb, ncp, KVW), lambda i: (i, 0, 0)),
                  pl.BlockSpec((NH, ncp), lambda i: (0, 0))],
        out_specs=[pl.BlockSpec((sb, NH, HEAD), lambda i: (i, 0, 0)),
                   pl.BlockSpec((sb, NH, s_pad), lambda i: (i, 0, 0))],
        out_shape=[jax.ShapeDtypeStruct((n, NH, HEAD), F32),
                   jax.ShapeDtypeStruct((n, NH, s_pad), F32)],
        compiler_params=_cparams(("parallel",)),
    )(q, kcc, vcc, bias_cs)


def _nsa_selwin_sample_kernel(pt_ref, *refs, n_pages, n_past_blk):
    k_pages = refs[:n_pages]
    v_pages = refs[n_pages:2 * n_pages]
    (q_ref, ksn_ref, vsn_ref, mask_ref, expand_ref, bs_ref, kw_ref, vw_ref, kwn_ref, vwn_ref, kwt_ref, vwt_ref,
     bw_ref, b0_ref, ocmp_ref, gate_ref, o_ref, kwo_ref, vwo_ref) = refs[2 * n_pages:]
    w_rows = kw_ref.shape[0]
    q = q_ref[...]
    q_bf = q.astype(BF16)
    b0 = b0_ref[...]

    def softmax_parts(s, kn_ref, new_extra):
        s_new = jnp.sum(q * kn_ref[...], axis=-1, keepdims=True) * SCALE + b0 + new_extra
        m = jnp.maximum(jnp.max(s, axis=-1, keepdims=True), s_new)
        p = jnp.exp(s - m)
        p_new = jnp.exp(s_new - m)
        return p.astype(BF16), p_new, jnp.sum(p, axis=-1, keepdims=True) + p_new

    mask = mask_ref[...]
    chosen = _dot(mask.astype(BF16), expand_ref[...])
    s = jnp.concatenate([_dot_nt(q_bf, kp[...].astype(BF16)) for kp in k_pages], axis=1) * SCALE
    s = s + bs_ref[...] + (chosen - 1.0) * 1e30
    new_chosen = mask[:, n_past_blk:n_past_blk + 1]
    p, p_new, denom = softmax_parts(s, ksn_ref, (new_chosen - 1.0) * 1e30)
    acc = p_new * vsn_ref[...]
    for pg, vp in enumerate(v_pages):
        acc = acc + _dot(p[:, pg * PAGE_ROWS:(pg + 1) * PAGE_ROWS], vp[...].astype(BF16))
    o_sel = acc / denom

    s = _dot_nt(q_bf, kw_ref[...].astype(BF16)) * SCALE + bw_ref[...]
    p, p_new, denom = softmax_parts(s, kwn_ref, 0.0)
    o_win = (p_new * vwn_ref[...] + _dot(p, vw_ref[...].astype(BF16))) / denom

    gate = _sigmoid(gate_ref[...])
    o = gate[:, 0:1] * ocmp_ref[...] + gate[:, 1:2] * o_sel + gate[:, 2:3] * o_win
    o_ref[...] = o.astype(o_ref.dtype)

    tail_rows = 2 * KVH
    is_new = lax.broadcasted_iota(jnp.int32, (tail_rows, 1), 0) >= KVH
    for src, tail, dst in ((kw_ref, kwt_ref, kwo_ref), (vw_ref, vwt_ref, vwo_ref)):
        moved = pltpu.roll(src[...], w_rows - KVH, axis=0)
        dst[0:w_rows - tail_rows, :] = moved[0:w_rows - tail_rows, :]
        dst[w_rows - tail_rows:w_rows, :] = jnp.where(is_new, tail[...], moved[w_rows - tail_rows:w_rows, :])


def _per_head(rows):
    n = rows.shape[0]
    return jnp.repeat(rows.reshape(n, KVH, HEAD), GRP, axis=1)


def _window_tail(rows):
    r = rows.reshape(rows.shape[0], KVH, HEAD)
    return jnp.concatenate([r, r], axis=1)


def nsa_selwin_sample(q, sel_k_rows, sel_v_rows, page_table, layer, ks_new, vs_new, mask, bias_ss,
                      win_k_rows, win_v_rows, kw_new, vw_new, bias_ws, bias0, o_cmp, gates, n_past_blk):
    n, n_pages = page_table.shape
    w_rows = win_k_rows.shape[2]
    s_pad = mask.shape[2]
    p_rows = n_pages * PAGE_ROWS
    row_blk = lax.broadcasted_iota(jnp.int32, (s_pad, p_rows), 1) // (SEL_BLOCK * KVH)
    expand = (row_blk == lax.broadcasted_iota(jnp.int32, (s_pad, p_rows), 0)).astype(BF16)

    def per_seq(*shape):
        return pl.BlockSpec((None,) + shape, lambda i, pt: (i,) + (0,) * len(shape))

    def shared(*shape):
        return pl.BlockSpec(shape, lambda i, pt: (0,) * len(shape))

    win_spec = pl.BlockSpec((None, None, w_rows, HEAD), lambda i, pt: (layer, i, 0, 0))
    head_vec = per_seq(NH, HEAD)
    tail = per_seq(2 * KVH, HEAD)
    return pl.pallas_call(
        functools.partial(_nsa_selwin_sample_kernel, n_pages=n_pages, n_past_blk=n_past_blk),
        name="nsa_selwin_sample",
        grid_spec=pltpu.PrefetchScalarGridSpec(
            num_scalar_prefetch=1,
            grid=(n,),
            in_specs=_page_specs(n_pages, layer) + _page_specs(n_pages, layer) + [
                head_vec, head_vec, head_vec, per_seq(NH, s_pad), shared(s_pad, p_rows), shared(NH, p_rows),
                win_spec, win_spec, head_vec, head_vec, tail, tail,
                shared(NH, w_rows), shared(NH, 1), head_vec, head_vec],
            out_specs=[head_vec, per_seq(w_rows, HEAD), per_seq(w_rows, HEAD)]),
        out_shape=[jax.ShapeDtypeStruct((n, NH, HEAD), BF16),
                   jax.ShapeDtypeStruct((n, w_rows, HEAD), F32),
                   jax.ShapeDtypeStruct((n, w_rows, HEAD), F32)],
        compiler_params=_cparams(("parallel",)),
    )(page_table, *([sel_k_rows] * n_pages), *([sel_v_rows] * n_pages),
      q, _per_head(ks_new), _per_head(vs_new), mask, expand, bias_ss,
      win_k_rows, win_v_rows, _per_head(kw_new), _per_head(vw_new), _window_tail(kw_new), _window_tail(vw_new),
      bias_ws, bias0, o_cmp, gates)


def _lookup(table_t, dist, valid=None):
    out = table_t[:, _bucket_np(dist)]
    if valid is not None:
        out = jnp.where(jnp.asarray(valid)[None], out, NEG)
    return out


def _prompt_bias(table_t):
    qi = np.arange(QB)[:, None]
    d_near = qi - (np.arange(NEAR_W)[None, :] - NEAR_BACK) * CMP_STRIDE - (CMP_BLOCK - 1)
    sat = table_t[:, N_BUCKETS - 1]
    near = _lookup(table_t, d_near) - sat[:, None, None]
    near = jnp.where(jnp.asarray(d_near >= 0)[None], near, 0.0)
    kj = np.arange(QB)[None, :]
    d0 = qi - kj
    const = jnp.broadcast_to(sat[:, None, None], (NH, QB, QB))
    tiles = jnp.stack([
        _lookup(table_t, d0, d0 >= 0),
        _lookup(table_t, d0 + QB),
        const, const,
        jnp.where(jnp.asarray(d0 + WINDOW < WINDOW)[None], const, NEG),
        jnp.full((NH, QB, QB), NEG, F32),
    ])
    return near, sat, tiles


def _sample_bias(table_t, past, w_len):
    ncp = past // CMP_STRIDE
    d_c = past - (np.arange(ncp) * CMP_STRIDE + CMP_BLOCK - 1)
    bias_cs = _lookup(table_t, d_c, d_c >= 0)
    own = (np.arange(NH)[:, None] // GRP) == (np.arange(KVH)[None, :])

    def per_row(dist, valid):
        b = _lookup(table_t, dist, valid)
        b = jnp.where(jnp.asarray(own)[:, None, :], b[:, :, None], NEG)
        return b.reshape(NH, -1)

    d_s = past - np.arange(past)
    d_w = w_len - np.arange(w_len)
    return bias_cs, per_row(d_s, d_s >= 0), per_row(d_w, d_w < WINDOW), table_t[:, 0:1]


def _split_w_in(w, d):
    wb = w.astype(BF16)
    o = 4 * d
    parts = {"zh": wb[:, :o], "q": wb[:, o:o + d]}
    o += d
    for name in ("kc", "vc", "ks", "vs", "kw", "vw"):
        parts[name] = wb[:, o:o + KVW]
        o += KVW
    wg = wb[:, o:o + 3 * NH]
    o += 3 * NH
    parts["gab"] = wb[:, o:o + 2 * d]
    parts["g_kv"] = jnp.pad(wg.reshape(-1, KVH, 3 * GRP), ((0, 0), (0, 0), (0, HEAD - 3 * GRP))).reshape(-1, KVH * HEAD)
    parts["g_head"] = jnp.pad(wg.reshape(-1, NH, 3), ((0, 0), (0, 0), (0, HEAD - 3))).reshape(-1, NH * HEAD)
    return parts


def _mixer_proj(h, wp, qk_gain_l, q_dtype):
    g = lambda i: qk_gain_l[i].reshape(1, HEAD)
    return dict(
        zh=mm(h, wp["zh"]),
        q=mm(h, wp["q"], out_dtype=q_dtype, head_gain=g(0)),
        kc=mm(h, wp["kc"]), vc=mm(h, wp["vc"]),
        ks=mm(h, wp["ks"], head_gain=g(2)), vs=mm(h, wp["vs"]),
        kw=mm(h, wp["kw"], head_gain=g(3)), vw=mm(h, wp["vw"]),
        gab=mm(h, wp["gab"]),
    )


def kernel(x_prompt, x_sample, cache_cmp_k, cache_cmp_v, cache_sel_k, cache_sel_v, cache_win_k, cache_win_v,
           state_hgrn, page_table, c_prompt, c_sample, w_ada, b_ada, norm_gain, w_in, lb_param, hgrn_gain,
           qk_gain, w_phi, rel_bias, w_proj_a, w_proj_b, w_out, w_ffn_in, w_ffn_out):
    nb, t_len, d = x_prompt.shape
    n = x_sample.shape[0]
    depth = w_in.shape[0]
    n_pages = page_table.shape[1]
    past = n_pages * PAGE
    w_len = cache_win_k.shape[2]
    n_slc_p = t_len // SEL_BLOCK
    n_past_blk = past // SEL_BLOCK
    n_slc_s = n_past_blk + 1
    s_pad = -(-n_slc_s // SEL_BLOCK) * SEL_BLOCK
    wb = min(WINDOW, t_len)

    sm = jax.nn.softmax(lb_param.astype(F32), axis=0)
    lower = jnp.cumsum(sm, axis=0) - sm[0]
    table_t = rel_bias.astype(F32).T
    near, sat, tiles = _prompt_bias(table_t)
    far = jnp.broadcast_to(sat[:, None, None], (NH, 1, t_len // CMP_STRIDE))
    bias_cs, bias_ss, bias_ws, bias0 = _sample_bias(table_t, past, w_len)
    qpos_p = jnp.arange(t_len, dtype=jnp.int32).reshape(1, t_len)
    qpos_s = jnp.full((1, n * KVH), past, jnp.int32)

    c_rows = nb + n
    c_pad = -(-c_rows // 8) * 8
    c_all = jnp.concatenate([c_prompt, c_sample, jnp.zeros((c_pad - c_rows, d), F32)], axis=0)

    xp = x_prompt.reshape(nb * t_len, d)
    xs = x_sample.reshape(n, d)
    outs_p = [[] for _ in range(7)]
    outs_s = [[] for _ in range(7)]

    for l in range(depth):
        wp = _split_w_in(w_in[l], d)
        w_pa, w_pb, w_o = w_proj_a[l].astype(BF16), w_proj_b[l].astype(BF16), w_out[l].astype(BF16)
        w1, w2 = w_ffn_in[l].astype(BF16), w_ffn_out[l].astype(BF16)
        wcat_k, wcat_v = (jnp.concatenate([w_phi[l, i, :, 0], w_phi[l, i, :, 1]], axis=-1).astype(BF16)
                          for i in range(2))
        ada = ada_all(c_all, w_ada[l].astype(BF16), b_ada[l].reshape(1, -1))
        hg_gain = hgrn_gain[l].reshape(1, HEAD)
        low = lower[l].reshape(1, d)
        gk_cmp = qk_gain[l, 1].reshape(1, HEAD)

        sh1, sc1, g1, sh2, sc2, g2 = (a.reshape(nb, 1, d) for a in jnp.split(ada[:nb], 6, axis=-1))
        h = modulate(xp, norm_gain[l, 0].reshape(1, d), sc1, sh1, t_len)
        z = _mixer_proj(h, wp, qk_gain[l], BF16)
        gates = mm(h, wp["g_kv"])
        o_a, s_new = hgrn_prompt(z["zh"], low, hg_gain, nb, t_len)
        half = lambda a: a.reshape(nb, t_len // CMP_STRIDE, CMP_STRIDE * KVW)
        kcc = compress(half(z["kc"]), wcat_k, gk_cmp, norm=True)
        vcc = compress(half(z["vc"]), wcat_v, gk_cmp)
        o_cmp, imp = nsa_cmp_prompt(z["q"], kcc, vcc, near, far, nb, t_len)
        sel = topk_mask(imp.reshape(nb * KVH, 1, n_slc_p, t_len), qpos_p, n_slc_p)
        sel = jnp.swapaxes(sel.reshape(nb, KVH, n_slc_p, t_len), 2, 3)
        seq = lambda a: a.reshape(nb, t_len, KVW)
        o_b = nsa_selwin_prompt(z["q"], seq(z["ks"]), seq(z["vs"]), seq(z["kw"]), seq(z["vw"]), sel, tiles,
                                o_cmp, gates, nb, t_len)
        mrg = merge_gate(o_a, o_b, w_pa, w_pb, z["gab"])
        xp = mm_residual(mrg, w_o, xp, g1, t_len)
        h = modulate(xp, norm_gain[l, 1].reshape(1, d), sc2, sh2, t_len)
        xp = mm_residual(ffn_in(h, w1), w2, xp, g2, t_len)
        rows = lambda a: a.reshape(nb, t_len, KVH, HEAD)
        for lst, a in zip(outs_p, (rows(z["kc"]), rows(z["vc"]), rows(z["ks"]), rows(z["vs"]),
                                   rows(z["kw"])[:, -wb:], rows(z["vw"])[:, -wb:], s_new)):
            lst.append(a)

        sh1, sc1, g1, sh2, sc2, g2 = jnp.split(ada[nb:nb + n], 6, axis=-1)
        h = modulate(xs, norm_gain[l, 0].reshape(1, d), sc1, sh1, 1)
        z = _mixer_proj(h, wp, qk_gain[l], F32)
        gates = mm(h, wp["g_head"]).reshape(n, NH, HEAD)
        o_a, s_new = hgrn_sample(z["zh"], low, hg_gain, state_hgrn, l)
        new = lambda a: a.reshape(n, 1, KVW)
        kcc = compress_sample(_rows_view(cache_cmp_k), page_table, l, wcat_k, gk_cmp,
                              new(tail_proj(z["kc"], wcat_k)), norm=True)
        vcc = compress_sample(_rows_view(cache_cmp_v), page_table, l, wcat_v, gk_cmp,
                              new(tail_proj(z["vc"], wcat_v)))
        q3 = z["q"].reshape(n, NH, HEAD)
        o_cmp, imp = nsa_cmp_sample(q3, kcc, vcc, bias_cs, s_pad)
        imp = imp.reshape(n, KVH, GRP, s_pad).transpose(2, 3, 0, 1).reshape(1, GRP, s_pad, n * KVH)
        sel = topk_mask(imp, qpos_s, n_slc_s)
        sel = jnp.repeat(sel.reshape(s_pad, n, KVH).transpose(1, 2, 0), GRP, axis=1)
        o_b, wk_new, wv_new = nsa_selwin_sample(
            q3, _rows_view(cache_sel_k), _rows_view(cache_sel_v), page_table, l, z["ks"], z["vs"], sel, bias_ss,
            _rows_view(cache_win_k), _rows_view(cache_win_v), z["kw"], z["vw"], bias_ws, bias0, o_cmp, gates,
            n_past_blk)
        mrg = merge_gate(o_a, o_b.reshape(n, d), w_pa, w_pb, z["gab"])
        xs = mm_residual(mrg, w_o, xs, g1, 1)
        h = modulate(xs, norm_gain[l, 1].reshape(1, d), sc2, sh2, 1)
        xs = mm_residual(ffn_in(h, w1), w2, xs, g2, 1)
        rows = lambda a: a.reshape(n, 1, KVH, HEAD)
        wrow = lambda a: a.reshape(n, w_len, KVH, HEAD)
        for lst, a in zip(outs_s, (rows(z["kc"]), rows(z["vc"]), rows(z["ks"]), rows(z["vs"]),
                                   wrow(wk_new), wrow(wv_new), s_new)):
            lst.append(a)

    return ((xp.reshape(nb, t_len, d), xs.reshape(n, 1, d))
            + tuple(jnp.stack(a) for a in outs_p) + tuple(jnp.stack(a) for a in outs_s))
```
